```python
import math
import jax, jax.numpy as jnp
from jax import lax
import numpy as np

D_MODEL = 1024
BATCH = 32
SEQ = 2048
DEPTH = 4

NORM_EPS = 1e-6
NEG_INF = -1e30
Q_BLOCK = 128
PLE_DIM = 256
D_FF = 2816
MLA_NOPE = 64
MLA_ROPE = 32
MLA_V = 64
MLA_HEADS = D_MODEL // MLA_V
MLA_Q_LORA = 384
MLA_KV_LORA = 256
ROPE_THETA = 10000.0
SWA_HEAD_DIM = 64
SWA_HEADS = D_MODEL // SWA_HEAD_DIM
SWA_KV_HEADS = 4
WINDOW = 128
DIFF_HEAD_DIM = 64
DIFF_HEADS = D_MODEL // (2 * DIFF_HEAD_DIM)
REL_BUCKETS = 32
REL_MAX_DIST = 128

IN_WIDTHS = (MLA_Q_LORA, MLA_KV_LORA, MLA_ROPE,
             SWA_HEADS * SWA_HEAD_DIM, SWA_KV_HEADS * SWA_HEAD_DIM, SWA_KV_HEADS * SWA_HEAD_DIM,
             DIFF_HEADS * 2 * DIFF_HEAD_DIM, DIFF_HEADS * 2 * DIFF_HEAD_DIM, DIFF_HEADS * 2 * DIFF_HEAD_DIM,
             D_MODEL, D_MODEL, D_MODEL)
IN_WIDTH = sum(IN_WIDTHS)
IN_SPLITS = tuple(int(v) for v in np.cumsum(IN_WIDTHS)[:-1])

kernel_name = 'hybrid_gated_mla_swa_diff_encoder'


def rms_norm(x, gain):
    x32 = x.astype(jnp.float32)
    y = x32 * lax.rsqrt(jnp.mean(x32 * x32, axis=-1, keepdims=True) + NORM_EPS)
    return (y * gain.astype(jnp.float32)).astype(x.dtype)


def swiglu(x, w_gate, w_up, w_down):
    return (jax.nn.silu(x @ w_gate) * (x @ w_up)) @ w_down


def t5_bucket(rel):
    nb = REL_BUCKETS // 2
    max_exact = nb // 2
    base = jnp.where(rel > 0, nb, 0)
    n = jnp.abs(rel)
    nf = jnp.maximum(n, 1).astype(jnp.float32)
    large = max_exact + (jnp.log(nf / max_exact) / math.log(REL_MAX_DIST / max_exact) * (nb - max_exact)).astype(jnp.int32)
    large = jnp.minimum(large, nb - 1)
    return base + jnp.where(n < max_exact, n, large)


def rel_bias(table, rel):
    return jnp.moveaxis(table[t5_bucket(rel)], -1, 0).astype(jnp.float32)


def rope_angles(seq_len):
    inv = ROPE_THETA ** (-jnp.arange(0, MLA_ROPE, 2, dtype=jnp.float32) / MLA_ROPE)
    ang = jnp.arange(seq_len, dtype=jnp.float32)[:, None] * inv[None, :]
    return jnp.cos(ang), jnp.sin(ang)


def apply_rope(x, cos, sin):
    x1, x2 = jnp.split(x.astype(jnp.float32), 2, axis=-1)
    return jnp.concatenate([x1 * cos - x2 * sin, x2 * cos + x1 * sin], axis=-1).astype(x.dtype)


def to_blocks(x):
    b, s = x.shape[:2]
    return jnp.moveaxis(x.reshape(b, s // Q_BLOCK, Q_BLOCK, *x.shape[2:]), 1, 0)


def from_blocks(y):
    y = jnp.moveaxis(y, 0, 1)
    return y.reshape(y.shape[0], y.shape[1] * y.shape[2], *y.shape[3:])


def mla_attention(c_q, c_kv, k_rope, q_norm, w_uq, kv_norm, w_ukv, cos, sin):
    b, s, _ = c_q.shape
    q = (rms_norm(c_q, q_norm) @ w_uq).reshape(b, s, MLA_HEADS, MLA_NOPE + MLA_ROPE)
    q_nope = q[..., :MLA_NOPE]
    q_rope = apply_rope(q[..., MLA_NOPE:], cos[:, None, :], sin[:, None, :])
    kv = (rms_norm(c_kv, kv_norm) @ w_ukv).reshape(b, s, MLA_HEADS, MLA_NOPE + MLA_V)
    k_nope, v = kv[..., :MLA_NOPE], kv[..., MLA_NOPE:]
    k_rope = apply_rope(k_rope, cos, sin)
    scale = (MLA_NOPE + MLA_ROPE) ** -0.5

    def block(args):
        qn, qr = args
        logits = (jnp.einsum('bqhd,bkhd->bhqk', qn, k_nope)
                  + jnp.einsum('bqhr,bkr->bhqk', qr, k_rope)).astype(jnp.float32) * scale
        probs = jax.nn.softmax(logits, axis=-1).astype(v.dtype)
        return jnp.einsum('bhqk,bkhd->bqhd', probs, v)

    out = lax.map(block, (to_blocks(q_nope), to_blocks(q_rope)))
    return from_blocks(out).reshape(b, s, MLA_HEADS * MLA_V)


def window_gqa(q, k, v, sink, bias_table):
    b, s = q.shape[:2]
    g = SWA_HEADS // SWA_KV_HEADS
    span = Q_BLOCK + 2 * WINDOW
    pad = ((0, 0), (WINDOW, WINDOW), (0, 0), (0, 0))
    kp, vp = jnp.pad(k, pad), jnp.pad(v, pad)
    qb = to_blocks(q.reshape(b, s, SWA_KV_HEADS, g, SWA_HEAD_DIM))
    scale = SWA_HEAD_DIM ** -0.5
    sink = sink.reshape(SWA_KV_HEADS, g).astype(jnp.float32)
    q_off = jnp.arange(Q_BLOCK)
    k_off = jnp.arange(span) - WINDOW

    def block(args):
        j, qj = args
        start = j * Q_BLOCK
        kj = lax.dynamic_slice_in_dim(kp, start, span, axis=1)
        vj = lax.dynamic_slice_in_dim(vp, start, span, axis=1)
        q_pos = start + q_off
        k_pos = start + k_off
        rel = k_pos[None, :] - q_pos[:, None]
        valid = (jnp.abs(rel) <= WINDOW) & (k_pos >= 0)[None, :] & (k_pos < s)[None, :]
        bias = rel_bias(bias_table, rel).reshape(SWA_KV_HEADS, g, Q_BLOCK, span)
        logits = jnp.einsum('bqkgd,bskd->bkgqs', qj, kj).astype(jnp.float32) * scale + bias
        logits = jnp.where(valid, logits, NEG_INF)
        sink_col = jnp.broadcast_to(sink[None, :, :, None, None], logits.shape[:-1] + (1,))
        probs = jax.nn.softmax(jnp.concatenate([logits, sink_col], axis=-1), axis=-1)[..., :span]
        return jnp.einsum('bkgqs,bskd->bqkgd', probs.astype(vj.dtype), vj)

    out = lax.map(block, (jnp.arange(s // Q_BLOCK), qb))
    return from_blocks(out).reshape(b, s, SWA_HEADS * SWA_HEAD_DIM)


def diff_attention(q, k, v, lq1, lk1, lq2, lk2, subln, lambda_init, bias_table):
    b, s = q.shape[:2]
    f32 = jnp.float32
    lam = (jnp.exp(jnp.sum(lq1.astype(f32) * lk1.astype(f32)))
           - jnp.exp(jnp.sum(lq2.astype(f32) * lk2.astype(f32))) + lambda_init)
    scale = DIFF_HEAD_DIM ** -0.5
    k_pos = jnp.arange(s)

    def block(args):
        j, qj = args
        q_pos = j * Q_BLOCK + jnp.arange(Q_BLOCK)
        bias = rel_bias(bias_table, k_pos[None, :] - q_pos[:, None])
        logits = jnp.einsum('bqhcd,bkhcd->bchqk', qj, k).astype(f32) * scale + bias
        probs = jax.nn.softmax(logits, axis=-1)
        attn = probs[:, 0] - lam * probs[:, 1]
        return jnp.einsum('bhqk,bkhe->bqhe', attn.astype(v.dtype), v)

    out = from_blocks(lax.map(block, (jnp.arange(s // Q_BLOCK), to_blocks(q))))
    out = rms_norm(out, subln) * (1.0 - lambda_init)
    return out.reshape(b, s, DIFF_HEADS * 2 * DIFF_HEAD_DIM)


def setup_inputs(seed: int = 0) -> dict:
    key = jax.random.key(seed)
    ks = iter(jax.random.split(key, 64))

    def normal(shape, scale):
        return jax.random.normal(next(ks), shape, jnp.float32) * scale

    def gain(shape):
        return 1.0 + normal(shape, 0.02)

    L, D, F = DEPTH, D_MODEL, D_FF
    dd = DIFF_HEAD_DIM
    return {
        'x': normal((BATCH, SEQ, D), 1.0),
        'p': normal((DEPTH, BATCH, SEQ, PLE_DIM), 1.0),
        'ffn1_norm': gain((L, D)),
        'ffn1_w_gate': normal((L, D, F), D ** -0.5),
        'ffn1_w_up': normal((L, D, F), D ** -0.5),
        'ffn1_w_down': normal((L, F, D), F ** -0.5),
        'mix_norm': gain((L, D)),
        'w_in': normal((L, D, IN_WIDTH), D ** -0.5),
        'mla_q_norm': gain((L, MLA_Q_LORA)),
        'mla_w_uq': normal((L, MLA_Q_LORA, MLA_HEADS * (MLA_NOPE + MLA_ROPE)), MLA_Q_LORA ** -0.5),
        'mla_kv_norm': gain((L, MLA_KV_LORA)),
        'mla_w_ukv': normal((L, MLA_KV_LORA, MLA_HEADS * (MLA_NOPE + MLA_V)), MLA_KV_LORA ** -0.5),
        'swa_sink': normal((L, SWA_HEADS), 0.5),
        'diff_lambda_q1': normal((L, dd), 0.1),
        'diff_lambda_k1': normal((L, dd), 0.1),
        'diff_lambda_q2': normal((L, dd), 0.1),
        'diff_lambda_k2': normal((L, dd), 0.1),
        'diff_subln': gain((L, 2 * dd)),
        'rel_table': normal((REL_BUCKETS, SWA_HEADS + DIFF_HEADS), 0.2),
        'w_out': normal((L, D, D), D ** -0.5),
        'ffn2_norm': gain((L, D)),
        'ffn2_w_gate': normal((L, D, F), D ** -0.5),
        'ffn2_w_up': normal((L, D, F), D ** -0.5),
        'ffn2_w_down': normal((L, F, D), F ** -0.5),
        'ple_norm': gain((L, D)),
        'ple_w_gate': normal((L, D, D), D ** -0.5),
        'ple_w_proj': normal((L, PLE_DIM, D), PLE_DIM ** -0.5),
        'final_norm': gain((D,)),
    }


def reference(x, p, ffn1_norm, ffn1_w_gate, ffn1_w_up, ffn1_w_down, mix_norm, w_in,
              mla_q_norm, mla_w_uq, mla_kv_norm, mla_w_ukv, swa_sink,
              diff_lambda_q1, diff_lambda_k1, diff_lambda_q2, diff_lambda_k2, diff_subln,
              rel_table, w_out, ffn2_norm, ffn2_w_gate, ffn2_w_up, ffn2_w_down,
              ple_norm, ple_w_gate, ple_w_proj, final_norm):
    b, s, _ = x.shape
    cos, sin = rope_angles(s)
    table_b = rel_table[:, :SWA_HEADS]
    table_c = rel_table[:, SWA_HEADS:]
    h = x
    for i in range(DEPTH):
        h = h + 0.5 * swiglu(rms_norm(h, ffn1_norm[i]), ffn1_w_gate[i], ffn1_w_up[i], ffn1_w_down[i])

        u = rms_norm(h, mix_norm[i])
        proj = u @ w_in[i]
        (c_q, c_kv, k_rope, q_b, k_b, v_b, q_c, k_c, v_c,
         g_a, g_b, g_c) = jnp.split(proj, IN_SPLITS, axis=-1)

        o_a = mla_attention(c_q, c_kv, k_rope, mla_q_norm[i], mla_w_uq[i],
                            mla_kv_norm[i], mla_w_ukv[i], cos, sin)
        o_b = window_gqa(q_b.reshape(b, s, SWA_HEADS, SWA_HEAD_DIM),
                         k_b.reshape(b, s, SWA_KV_HEADS, SWA_HEAD_DIM),
                         v_b.reshape(b, s, SWA_KV_HEADS, SWA_HEAD_DIM),
                         swa_sink[i], table_b)
        lambda_init = 0.8 - 0.6 * math.exp(-0.3 * i)
        o_c = diff_attention(q_c.reshape(b, s, DIFF_HEADS, 2, DIFF_HEAD_DIM),
                             k_c.reshape(b, s, DIFF_HEADS, 2, DIFF_HEAD_DIM),
                             v_c.reshape(b, s, DIFF_HEADS, 2 * DIFF_HEAD_DIM),
                             diff_lambda_q1[i], diff_lambda_k1[i], diff_lambda_q2[i], diff_lambda_k2[i],
                             diff_subln[i], lambda_init, table_c)

        merged = jax.nn.sigmoid(g_a) * o_a + jax.nn.sigmoid(g_b) * o_b + jax.nn.sigmoid(g_c) * o_c
        h = h + merged @ w_out[i]

        h = h + 0.5 * swiglu(rms_norm(h, ffn2_norm[i]), ffn2_w_gate[i], ffn2_w_up[i], ffn2_w_down[i])

        gate = jax.nn.sigmoid(rms_norm(h, ple_norm[i]) @ ple_w_gate[i])
        h = h + gate * (p[i] @ ple_w_proj[i])
    return rms_norm(h, final_norm)
```

```python
import functools
import math

import jax
import jax.numpy as jnp
import numpy as np
from jax import lax
from jax.experimental import pallas as pl
from jax.experimental.pallas import tpu as pltpu

F32 = jnp.float32
BF16 = jnp.bfloat16

D_MODEL = 1024
D_FF = 2816
PLE_DIM = 256
NORM_EPS = 1e-6
NEG_INF = -1e30
ROPE_THETA = 10000.0
MLA_HEADS = 16
MLA_NOPE = 64
MLA_ROPE = 32
MLA_V = 64
MLA_Q_LORA = 384
MLA_KV_LORA = 256
SWA_HEADS = 16
SWA_KV_HEADS = 4
SWA_HEAD_DIM = 64
WINDOW = 128
Q_BLOCK = 128
DIFF_HEADS = 8
DIFF_HEAD_DIM = 64
REL_BUCKETS = 32
REL_MAX_DIST = 128

LANES = 128
VMEM_LIMIT = 56 * 1024 * 1024

FFN_ROWS = 512
PROJ_ROWS = 256
MERGE_ROWS = 512
PLE_ROWS = 512
MLA_Q_ROWS = 512
DIFF_Q_ROWS = 256


def _params(n_axes):
    return pltpu.CompilerParams(dimension_semantics=("arbitrary",) * n_axes,
                                vmem_limit_bytes=VMEM_LIMIT)


def _resident(shape, index_map):
    return pl.BlockSpec(shape, index_map, pipeline_mode=pl.Buffered(1))


def _rms(x, gain):
    return x * lax.rsqrt(jnp.mean(x * x, axis=-1, keepdims=True) + NORM_EPS) * gain


def _dot(a, b):
    return jnp.dot(a, b, preferred_element_type=F32)


def _dot_nt(a, b):
    return lax.dot_general(a, b, (((1,), (1,)), ((), ())), preferred_element_type=F32)


def _ffn_body(h_ref, gain_ref, wg_ref, wu_ref, wd_ref, o_ref):
    h = h_ref[...]
    n = _rms(h, gain_ref[...]).astype(BF16)
    g = _dot(n, wg_ref[...])
    u = _dot(n, wu_ref[...])
    a = (g * jax.nn.sigmoid(g) * u).astype(BF16)
    o_ref[...] = h + 0.5 * _dot(a, wd_ref[...])


def _ffn(h, layer, gain, wg, wu, wd):
    n = h.shape[0]
    tm = FFN_ROWS
    row = lambda r: (r, 0)
    lay3 = lambda r: (layer, 0, 0)
    return pl.pallas_call(
        _ffn_body,
        grid=(n // tm,),
        in_specs=[pl.BlockSpec((tm, D_MODEL), row),
                  _resident((None, 1, D_MODEL), lay3),
                  _resident((None, D_MODEL, D_FF), lay3),
                  _resident((None, D_MODEL, D_FF), lay3),
                  _resident((None, D_FF, D_MODEL), lay3)],
        out_specs=pl.BlockSpec((tm, D_MODEL), row),
        out_shape=jax.ShapeDtypeStruct(h.shape, F32),
        compiler_params=_params(1),
        name="ffn",
    )(h, gain, wg, wu, wd)


def _rope(x, cos_t, sin_lo, sin_hi):
    return (x * cos_t + pltpu.roll(x, LANES - MLA_ROPE // 2, 1) * sin_lo
            + pltpu.roll(x, MLA_ROPE // 2, 1) * sin_hi)


def _proj_body(h_ref, gain_ref, wlat_ref, wb_ref, wc_ref, wgate_ref, qn_ref, kvn_ref,
               wuq_ref, wuk_ref, wuv_ref, cos_ref, slo_ref, shi_ref,
               qa_ref, ka_ref, va_ref, qb_ref, kb_ref, vb_ref, qc_ref, kc_ref, vc_ref, sg_ref):
    u = _rms(h_ref[...], gain_ref[...]).astype(BF16)

    lat = _dot(u, wlat_ref[...])
    cq = _rms(lat[:, :MLA_Q_LORA], qn_ref[...]).astype(BF16)
    ckv = _rms(lat[:, MLA_Q_LORA:MLA_Q_LORA + MLA_KV_LORA], kvn_ref[...]).astype(BF16)
    kr = lat[:, MLA_Q_LORA + MLA_KV_LORA:]

    cos_t, sin_lo, sin_hi = cos_ref[...], slo_ref[...], shi_ref[...]
    lane = lax.broadcasted_iota(jnp.int32, cos_t.shape, 1)
    scale = (MLA_NOPE + MLA_ROPE) ** -0.5
    q_cos = (cos_t + jnp.where(lane < MLA_NOPE, 1.0, 0.0)) * scale
    q_lo, q_hi = sin_lo * scale, sin_hi * scale
    kr_rot = _rope(kr, cos_t, sin_lo, sin_hi)

    q = _dot(cq, wuq_ref[...])
    kn = _dot(ckv, wuk_ref[...])
    for hd in range(MLA_HEADS):
        sl = slice(hd * LANES, (hd + 1) * LANES)
        qa_ref[:, sl] = _rope(q[:, sl], q_cos, q_lo, q_hi).astype(BF16)
        ka_ref[:, sl] = (kn[:, sl] + kr_rot).astype(BF16)
    va_ref[...] = _dot(ckv, wuv_ref[...]).astype(BF16)

    pb = _dot(u, wb_ref[...])
    qb_ref[...] = pb[:, :1024].astype(BF16)
    kb_ref[...] = pb[:, 1024:1536].astype(BF16)
    vb_ref[...] = pb[:, 1536:].astype(BF16)
    pc = _dot(u, wc_ref[...])
    qc_ref[...] = pc[:, :1024].astype(BF16)
    kc_ref[...] = pc[:, 1024:2048].astype(BF16)
    vc_ref[...] = pc[:, 2048:].astype(BF16)
    sg_ref[...] = jax.nn.sigmoid(_dot(u, wgate_ref[...])).astype(BF16)


def _proj(h, layer, seq, gain, wlat, wb, wc, wgate, qn, kvn, wuq, wuk, wuv, cos_t, sin_lo, sin_hi):
    n = h.shape[0]
    tm = PROJ_ROWS
    row = lambda r: (r, 0)
    lay3 = lambda r: (layer, 0, 0)
    pos = lambda r: (r % (seq // tm), 0)
    widths = (2048, 2048, 1024, 1024, 512, 512, 1024, 1024, 1024, 3072)
    res = lambda a: _resident((None,) + a.shape[1:], lay3)
    return pl.pallas_call(
        _proj_body,
        grid=(n // tm,),
        in_specs=[pl.BlockSpec((tm, D_MODEL), row), res(gain), res(wlat), res(wb), res(wc), res(wgate),
                  res(qn), res(kvn), res(wuq), res(wuk), res(wuv),
                  pl.BlockSpec((tm, LANES), pos), pl.BlockSpec((tm, LANES), pos),
                  pl.BlockSpec((tm, LANES), pos)],
        out_specs=[pl.BlockSpec((tm, w), row) for w in widths],
        out_shape=[jax.ShapeDtypeStruct((n, w), BF16) for w in widths],
        compiler_params=_params(1),
        name="proj",
    )(h, gain, wlat, wb, wc, wgate, qn, kvn, wuq, wuk, wuv, cos_t, sin_lo, sin_hi)


def _low_half(shape):
    return lax.broadcasted_iota(jnp.int32, shape, len(shape) - 1) < LANES // 2


def _half(x, low, t):
    return jnp.where(low if t == 0 else jnp.logical_not(low), x, jnp.zeros_like(x))


def _mla_body(q_ref, k_ref, v_ref, o_ref):
    v = v_ref[...]
    outs = []
    for j in range(2):
        sl = slice(j * LANES, (j + 1) * LANES)
        s = _dot_nt(q_ref[:, sl], k_ref[:, sl])
        m = jnp.max(s, axis=-1, keepdims=True)
        p = jnp.exp(s - m)
        l = jnp.sum(p, axis=-1, keepdims=True)
        outs.append(_dot(p.astype(BF16), v) / l)
    o_ref[...] = jnp.where(_low_half(outs[0].shape), outs[0], outs[1]).astype(BF16)


def _mla(qa, ka, va):
    b, s, _ = va.shape
    tq = MLA_Q_ROWS
    return pl.pallas_call(
        _mla_body,
        grid=(b, MLA_HEADS // 2, s // tq),
        in_specs=[pl.BlockSpec((None, tq, 2 * LANES), lambda bi, hp, qi: (bi, qi, hp)),
                  pl.BlockSpec((None, s, 2 * LANES), lambda bi, hp, qi: (bi, 0, hp)),
                  pl.BlockSpec((None, s, LANES), lambda bi, hp, qi: (bi, 0, hp))],
        out_specs=pl.BlockSpec((None, tq, LANES), lambda bi, hp, qi: (bi, qi, hp)),
        out_shape=jax.ShapeDtypeStruct(va.shape, BF16),
        compiler_params=_params(3),
        name="mla",
    )(qa, ka, va)


def _swa_body(sink_ref, q_ref, k_ref, v_ref, bias_ref, o_ref):
    slot = pl.program_id(1)
    n_blocks = q_ref.shape[0] // Q_BLOCK
    low = _low_half((Q_BLOCK, LANES))
    for j in range(n_blocks):
        q = q_ref[j * Q_BLOCK:(j + 1) * Q_BLOCK, :]
        lo = max(j - 1, 0) * Q_BLOCK
        hi = min(j + 2, n_blocks) * Q_BLOCK
        c0 = lo - (j - 1) * Q_BLOCK
        k = k_ref[lo:hi, :]
        v = v_ref[lo:hi, :]
        outs = []
        for t in range(2):
            qm = _half(q, low, t)
            s = _dot_nt(qm, k) + bias_ref[t, :, c0:c0 + (hi - lo)]
            sink = sink_ref[2 * slot + t]
            m = jnp.maximum(jnp.max(s, axis=-1, keepdims=True), sink)
            p = jnp.exp(s - m)
            l = jnp.sum(p, axis=-1, keepdims=True) + jnp.exp(sink - m)
            outs.append(_dot(p.astype(BF16), v) / l)
        o_ref[j * Q_BLOCK:(j + 1) * Q_BLOCK, :] = jnp.where(low, outs[0], outs[1]).astype(BF16)


def _swa(sink, qb, kb, vb, bias):
    b, s, _ = qb.shape
    n_slots = SWA_HEADS // 2
    per_kv = n_slots // SWA_KV_HEADS
    return pl.pallas_call(
        _swa_body,
        grid=(b, n_slots),
        in_specs=[pl.BlockSpec(memory_space=pltpu.SMEM),
                  pl.BlockSpec((None, s, LANES), lambda bi, sl: (bi, 0, sl)),
                  pl.BlockSpec((None, s, LANES), lambda bi, sl: (bi, 0, sl // per_kv)),
                  pl.BlockSpec((None, s, LANES), lambda bi, sl: (bi, 0, sl // per_kv)),
                  pl.BlockSpec((2, Q_BLOCK, 3 * Q_BLOCK), lambda bi, sl: (sl, 0, 0))],
        out_specs=pl.BlockSpec((None, s, LANES), lambda bi, sl: (bi, 0, sl)),
        out_shape=jax.ShapeDtypeStruct(qb.shape, BF16),
        compiler_params=_params(2),
        name="swa",
    )(sink, qb, kb, vb, bias)


def _diff_body(lam_ref, q_ref, k_ref, v_ref, bias_ref, subln_ref, o_ref, *, lambda_init):
    qi = pl.program_id(2)
    q = q_ref[...]
    k = k_ref[...]
    tc = DIFF_Q_ROWS
    low = _low_half(q.shape)
    lv = lam_ref[...]
    lam = (jnp.exp(jnp.sum(lv[0:1] * lv[1:2], axis=-1, keepdims=True))
           - jnp.exp(jnp.sum(lv[2:3] * lv[3:4], axis=-1, keepdims=True)) + lambda_init)
    probs = []
    for t in range(2):
        qm = _half(q, low, t)
        s = _dot_nt(qm, k)
        pieces = []
        for c in range(s.shape[1] // tc):
            which = jnp.clip(c - qi, -2, 2) + 2
            pieces.append(s[:, c * tc:(c + 1) * tc] + bias_ref[which])
        s = jnp.concatenate(pieces, axis=1)
        m = jnp.max(s, axis=-1, keepdims=True)
        p = jnp.exp(s - m)
        l = jnp.sum(p, axis=-1, keepdims=True)
        probs.append((p, l))
    attn = probs[0][0] * (1.0 / probs[0][1]) - probs[1][0] * (lam / probs[1][1])
    o = _dot(attn.astype(BF16), v_ref[...])
    o_ref[...] = (_rms(o, subln_ref[...]) * (1.0 - lambda_init)).astype(BF16)


def _diff(layer, lambda_init, lam_vecs, qc, kc, vc, bias, subln):
    b, s, _ = qc.shape
    tq = DIFF_Q_ROWS
    return pl.pallas_call(
        functools.partial(_diff_body, lambda_init=lambda_init),
        grid=(DIFF_HEADS, b, s // tq),
        in_specs=[pl.BlockSpec((None, 4, DIFF_HEAD_DIM), lambda hd, bi, qi: (layer, 0, 0)),
                  pl.BlockSpec((None, tq, LANES), lambda hd, bi, qi: (bi, qi, hd)),
                  pl.BlockSpec((None, s, LANES), lambda hd, bi, qi: (bi, 0, hd)),
                  pl.BlockSpec((None, s, LANES), lambda hd, bi, qi: (bi, 0, hd)),
                  pl.BlockSpec((None, 5, tq, tq), lambda hd, bi, qi: (hd, 0, 0, 0)),
                  pl.BlockSpec((None, 1, LANES), lambda hd, bi, qi: (layer, 0, 0))],
        out_specs=pl.BlockSpec((None, tq, LANES), lambda hd, bi, qi: (bi, qi, hd)),
        out_shape=jax.ShapeDtypeStruct(qc.shape, BF16),
        compiler_params=_params(3),
        name="diff",
    )(lam_vecs, qc, kc, vc, bias, subln)


def _merge_body(h_ref, sg_ref, oa_ref, ob_ref, oc_ref, w_ref, o_ref):
    d = D_MODEL
    merged = (sg_ref[:, :d].astype(F32) * oa_ref[...].astype(F32)
              + sg_ref[:, d:2 * d].astype(F32) * ob_ref[...].astype(F32)
              + sg_ref[:, 2 * d:].astype(F32) * oc_ref[...].astype(F32))
    o_ref[...] = h_ref[...] + _dot(merged.astype(BF16), w_ref[...])


def _merge(h, layer, sg, oa, ob, oc, w_out):
    n = h.shape[0]
    tm = MERGE_ROWS
    row = lambda r: (r, 0)
    return pl.pallas_call(
        _merge_body,
        grid=(n // tm,),
        in_specs=[pl.BlockSpec((tm, D_MODEL), row), pl.BlockSpec((tm, 3 * D_MODEL), row),
                  pl.BlockSpec((tm, D_MODEL), row), pl.BlockSpec((tm, D_MODEL), row),
                  pl.BlockSpec((tm, D_MODEL), row),
                  _resident((None, D_MODEL, D_MODEL), lambda r: (layer, 0, 0))],
        out_specs=pl.BlockSpec((tm, D_MODEL), row),
        out_shape=jax.ShapeDtypeStruct(h.shape, F32),
        compiler_params=_params(1),
        name="merge",
    )(h, sg, oa, ob, oc, w_out)


def _ple_body(h_ref, p_ref, gain_ref, wg_ref, wp_ref, fin_ref, o_ref, *, final):
    h = h_ref[...]
    gate = jax.nn.sigmoid(_dot(_rms(h, gain_ref[...]).astype(BF16), wg_ref[...]))
    h = h + gate * _dot(p_ref[...].astype(BF16), wp_ref[...])
    o_ref[...] = _rms(h, fin_ref[...]) if final else h


def _ple(h, p, layer, gain, wg, wp, final_norm, final):
    n = h.shape[0]
    tm = PLE_ROWS
    row = lambda r: (r, 0)
    lay3 = lambda r: (layer, 0, 0)
    return pl.pallas_call(
        functools.partial(_ple_body, final=final),
        grid=(n // tm,),
        in_specs=[pl.BlockSpec((tm, D_MODEL), row),
                  pl.BlockSpec((None, tm, PLE_DIM), lambda r: (layer, r, 0)),
                  _resident((None, 1, D_MODEL), lay3),
                  _resident((None, D_MODEL, D_MODEL), lay3),
                  _resident((None, PLE_DIM, D_MODEL), lay3),
                  _resident((1, D_MODEL), lambda r: (0, 0))],
        out_specs=pl.BlockSpec((tm, D_MODEL), row),
        out_shape=jax.ShapeDtypeStruct(h.shape, F32),
        compiler_params=_params(1),
        name="ple",
    )(h, p, gain, wg, wp, final_norm)


def _t5_bucket_np(rel):
    nb = REL_BUCKETS // 2
    max_exact = nb // 2
    base = np.where(rel > 0, nb, 0)
    n = np.abs(rel)
    nf = np.maximum(n, 1).astype(np.float64)
    large = max_exact + (np.log(nf / max_exact) / math.log(REL_MAX_DIST / max_exact)
                         * (nb - max_exact)).astype(np.int64)
    large = np.minimum(large, nb - 1)
    return (base + np.where(n < max_exact, n, large)).astype(np.int32)


def _bias_tiles(rel_table):
    table_b = rel_table[:, :SWA_HEADS].astype(F32)
    table_c = rel_table[:, SWA_HEADS:].astype(F32)
    rel = (np.arange(3 * Q_BLOCK)[None, :] - WINDOW) - np.arange(Q_BLOCK)[:, None]
    bias_b = jnp.moveaxis(table_b[_t5_bucket_np(rel)], -1, 0)
    bias_b = jnp.where(jnp.asarray(np.abs(rel) <= WINDOW)[None], bias_b, NEG_INF)
    t = DIFF_Q_ROWS
    tiles = []
    for d in range(-2, 3):
        rel = (np.arange(t)[None, :] + d * t) - np.arange(t)[:, None]
        tiles.append(jnp.moveaxis(table_c[_t5_bucket_np(rel)], -1, 0))
    bias_c = jnp.stack(tiles, axis=1)
    return bias_b, bias_c


def _rope_tables(seq):
    half = MLA_ROPE // 2
    inv = ROPE_THETA ** (-jnp.arange(0, MLA_ROPE, 2, dtype=F32) / MLA_ROPE)
    ang = jnp.arange(seq, dtype=F32)[:, None] * inv[None, :]
    cos, sin = jnp.cos(ang), jnp.sin(ang)
    z = lambda w: jnp.zeros((seq, w), F32)
    cos_t = jnp.concatenate([z(MLA_NOPE), cos, cos, z(LANES - MLA_NOPE - MLA_ROPE)], axis=1)
    sin_lo = jnp.concatenate([z(MLA_NOPE), -sin, z(LANES - MLA_NOPE - half)], axis=1)
    sin_hi = jnp.concatenate([z(MLA_NOPE + half), sin, z(LANES - MLA_NOPE - MLA_ROPE)], axis=1)
    return cos_t, sin_lo, sin_hi


def _dup_heads(w, heads, dim):
    w4 = w.reshape(w.shape[:2] + (heads, dim))
    return jnp.concatenate([w4, w4], axis=-1).reshape(w.shape[:2] + (heads * 2 * dim,))


def kernel(x, p, ffn1_norm, ffn1_w_gate, ffn1_w_up, ffn1_w_down, mix_norm, w_in, mla_q_norm, mla_w_uq, mla_kv_norm, mla_w_ukv, swa_sink, diff_lambda_q1, diff_lambda_k1, diff_lambda_q2, diff_lambda_k2, diff_subln, rel_table, w_out, ffn2_norm, ffn2_w_gate, ffn2_w_up, ffn2_w_down, ple_norm, ple_w_gate, ple_w_proj, final_norm):
    b, s, d = x.shape
    depth = w_in.shape[0]
    n = b * s
    bf = lambda a: a.astype(BF16)
    vec = lambda a: a.astype(F32)[:, None, :]

    o = np.cumsum((0, MLA_Q_LORA, MLA_KV_LORA, MLA_ROPE, 1024, 256, 256, 1024, 1024, 1024, 3072))
    col = lambda i: w_in[:, :, o[i]:o[i + 1]]
    zeros = lambda w: jnp.zeros((depth, d, w), w_in.dtype)
    wlat = bf(jnp.concatenate([col(0), col(1), zeros(MLA_NOPE), col(2),
                               zeros(LANES - MLA_NOPE - MLA_ROPE)], axis=-1))
    wb = bf(jnp.concatenate([col(3) * SWA_HEAD_DIM ** -0.5,
                             _dup_heads(col(4), SWA_KV_HEADS, SWA_HEAD_DIM),
                             _dup_heads(col(5), SWA_KV_HEADS, SWA_HEAD_DIM)], axis=-1))
    wc = bf(jnp.concatenate([col(6) * DIFF_HEAD_DIM ** -0.5, col(7), col(8)], axis=-1))
    wgate = bf(col(9))
    uq = mla_w_uq.reshape(depth, MLA_Q_LORA, MLA_HEADS, MLA_NOPE + MLA_ROPE)
    wuq = bf(jnp.pad(uq, ((0, 0),) * 3 + ((0, LANES - MLA_NOPE - MLA_ROPE),))
             .reshape(depth, MLA_Q_LORA, MLA_HEADS * LANES))
    ukv = mla_w_ukv.reshape(depth, MLA_KV_LORA, MLA_HEADS, MLA_NOPE + MLA_V)
    wuk = bf(jnp.pad(ukv[..., :MLA_NOPE], ((0, 0),) * 3 + ((0, LANES - MLA_NOPE),))
             .reshape(depth, MLA_KV_LORA, MLA_HEADS * LANES))
    wuv = bf(ukv[..., MLA_NOPE:].reshape(depth, MLA_KV_LORA, MLA_HEADS * MLA_V))

    cos_t, sin_lo, sin_hi = _rope_tables(s)
    bias_b, bias_c = _bias_tiles(rel_table)
    lam_vecs = jnp.stack([diff_lambda_q1, diff_lambda_k1, diff_lambda_q2, diff_lambda_k2],
                         axis=1).astype(F32)
    f1 = (vec(ffn1_norm), bf(ffn1_w_gate), bf(ffn1_w_up), bf(ffn1_w_down))
    f2 = (vec(ffn2_norm), bf(ffn2_w_gate), bf(ffn2_w_up), bf(ffn2_w_down))
    mixn, qn, kvn, subln, plen = (vec(mix_norm), vec(mla_q_norm), vec(mla_kv_norm),
                                  vec(diff_subln), vec(ple_norm))
    wo, wpg, wpp = bf(w_out), bf(ple_w_gate), bf(ple_w_proj)
    fin = final_norm.astype(F32)[None, :]
    p2 = p.reshape(depth, n, PLE_DIM)

    h = x.reshape(n, d).astype(F32)
    for i in range(depth):
        h = _ffn(h, i, *f1)
        qa, ka, va, qb, kb, vb, qc, kc, vc, sg = _proj(
            h, i, s, mixn, wlat, wb, wc, wgate, qn, kvn, wuq, wuk, wuv, cos_t, sin_lo, sin_hi)
        r3 = lambda a: a.reshape(b, s, a.shape[-1])
        oa = _mla(r3(qa), r3(ka), r3(va))
        ob = _swa(swa_sink[i].astype(F32), r3(qb), r3(kb), r3(vb), bias_b)
        lambda_init = 0.8 - 0.6 * math.exp(-0.3 * i)
        oc = _diff(i, lambda_init, lam_vecs, r3(qc), r3(kc), r3(vc), bias_c, subln)
        h = _merge(h, i, sg, oa.reshape(n, d), ob.reshape(n, d), oc.reshape(n, d), wo)
        h = _ffn(h, i, *f2)
        h = _ple(h, p2, i, plen, wpg, wpp, fin, final=(i == depth - 1))
    return h.reshape(b, s, d).astype(x.dtype)
```

```python
import functools
import math

import jax
import jax.numpy as jnp
import numpy as np
from jax import lax
from jax.experimental import pallas as pl
from jax.experimental.pallas import tpu as pltpu

F32 = jnp.float32
BF16 = jnp.bfloat16

D_MODEL = 1024
D_FF = 2816
PLE_DIM = 256
NORM_EPS = 1e-6
NEG_INF = -1e30
ROPE_THETA = 10000.0
MLA_HEADS = 16
MLA_NOPE = 64
MLA_ROPE = 32
MLA_V = 64
MLA_Q_LORA = 384
MLA_KV_LORA = 256
SWA_HEADS = 16
SWA_KV_HEADS = 4
SWA_HEAD_DIM = 64
WINDOW = 128
Q_BLOCK = 128
DIFF_HEADS = 8
DIFF_HEAD_DIM = 64
REL_BUCKETS = 32
REL_MAX_DIST = 128
LOG2E = math.log2(math.e)

LANES = 128
VMEM_LIMIT = 56 * 1024 * 1024

FFN_ROWS = 512
PROJ_ROWS = 256
MERGE_ROWS = 512
PLE_ROWS = 512
MLA_Q_ROWS = 512
MLA_STEP_ROWS = 1024
DIFF_Q_ROWS = 256
DIFF_STEP_ROWS = 1024


def _params(n_axes):
    return pltpu.CompilerParams(dimension_semantics=("arbitrary",) * n_axes,
                                vmem_limit_bytes=VMEM_LIMIT)


def _resident(shape, index_map):
    return pl.BlockSpec(shape, index_map, pipeline_mode=pl.Buffered(1))


def _rms(x, gain):
    return x * lax.rsqrt(jnp.mean(x * x, axis=-1, keepdims=True) + NORM_EPS) * gain


def _dot(a, b):
    return jnp.dot(a, b, preferred_element_type=F32)


def _dot_nt(a, b):
    return lax.dot_general(a, b, (((1,), (1,)), ((), ())), preferred_element_type=F32)


def _ffn_body(h_ref, gain_ref, wg_ref, wu_ref, wd_ref, o_ref):
    h = h_ref[...]
    n = _rms(h, gain_ref[...]).astype(BF16)
    g = _dot(n, wg_ref[...])
    u = _dot(n, wu_ref[...])
    a = (g * jax.nn.sigmoid(g) * u).astype(BF16)
    o_ref[...] = h + 0.5 * _dot(a, wd_ref[...])


def _ffn(h, layer, gain, wg, wu, wd):
    n = h.shape[0]
    tm = FFN_ROWS
    row = lambda r: (r, 0)
    lay3 = lambda r: (layer, 0, 0)
    return pl.pallas_call(
        _ffn_body,
        grid=(n // tm,),
        in_specs=[pl.BlockSpec((tm, D_MODEL), row),
                  _resident((None, 1, D_MODEL), lay3),
                  _resident((None, D_MODEL, D_FF), lay3),
                  _resident((None, D_MODEL, D_FF), lay3),
                  _resident((None, D_FF, D_MODEL), lay3)],
        out_specs=pl.BlockSpec((tm, D_MODEL), row),
        out_shape=jax.ShapeDtypeStruct(h.shape, F32),
        compiler_params=_params(1),
        name="ffn",
    )(h, gain, wg, wu, wd)


def _rope(x, cos_t, sin_lo, sin_hi):
    return (x * cos_t + pltpu.roll(x, LANES - MLA_ROPE // 2, 1) * sin_lo
            + pltpu.roll(x, MLA_ROPE // 2, 1) * sin_hi)


def _proj_body(h_ref, gain_ref, wlat_ref, wb_ref, wc_ref, wgate_ref, qn_ref, kvn_ref,
               wuq_ref, wuk_ref, wuv_ref, cos_ref, slo_ref, shi_ref,
               qa_ref, ka_ref, va_ref, qb_ref, kb_ref, vb_ref, qc_ref, kc_ref, vc_ref, sg_ref):
    u = _rms(h_ref[...], gain_ref[...]).astype(BF16)

    lat = _dot(u, wlat_ref[...])
    cq = _rms(lat[:, :MLA_Q_LORA], qn_ref[...]).astype(BF16)
    ckv = _rms(lat[:, MLA_Q_LORA:MLA_Q_LORA + MLA_KV_LORA], kvn_ref[...]).astype(BF16)
    kr = lat[:, MLA_Q_LORA + MLA_KV_LORA:]

    cos_t, sin_lo, sin_hi = cos_ref[...], slo_ref[...], shi_ref[...]
    lane = lax.broadcasted_iota(jnp.int32, cos_t.shape, 1)
    scale = (MLA_NOPE + MLA_ROPE) ** -0.5 * LOG2E
    q_cos = (cos_t + jnp.where(lane < MLA_NOPE, 1.0, 0.0)) * scale
    q_lo, q_hi = sin_lo * scale, sin_hi * scale
    kr_rot = _rope(kr, cos_t, sin_lo, sin_hi)

    q = _dot(cq, wuq_ref[...])
    kn = _dot(ckv, wuk_ref[...])
    for hd in range(MLA_HEADS):
        sl = slice(hd * LANES, (hd + 1) * LANES)
        qa_ref[:, sl] = _rope(q[:, sl], q_cos, q_lo, q_hi).astype(BF16)
        ka_ref[:, sl] = (kn[:, sl] + kr_rot).astype(BF16)
    va_ref[...] = _dot(ckv, wuv_ref[...]).astype(BF16)

    pb = _dot(u, wb_ref[...])
    qb_ref[...] = (pb[:, :1024] * (SWA_HEAD_DIM ** -0.5 * LOG2E)).astype(BF16)
    kb_ref[...] = pb[:, 1024:1536].astype(BF16)
    vb_ref[...] = pb[:, 1536:].astype(BF16)
    pc = _dot(u, wc_ref[...])
    qc_ref[...] = (pc[:, :1024] * (DIFF_HEAD_DIM ** -0.5 * LOG2E)).astype(BF16)
    kc_ref[...] = pc[:, 1024:2048].astype(BF16)
    vc_ref[...] = pc[:, 2048:].astype(BF16)
    sg_ref[...] = jax.nn.sigmoid(_dot(u, wgate_ref[...])).astype(BF16)


def _proj(h, layer, seq, gain, wlat, wb, wc, wgate, qn, kvn, wuq, wuk, wuv, cos_t, sin_lo, sin_hi):
    n = h.shape[0]
    tm = PROJ_ROWS
    row = lambda r: (r, 0)
    lay3 = lambda r: (layer, 0, 0)
    pos = lambda r: (r % (seq // tm), 0)
    widths = (2048, 2048, 1024, 1024, 512, 512, 1024, 1024, 1024, 3072)
    res = lambda a: _resident((None,) + a.shape[1:], lay3)
    return pl.pallas_call(
        _proj_body,
        grid=(n // tm,),
        in_specs=[pl.BlockSpec((tm, D_MODEL), row), res(gain), res(wlat), res(wb), res(wc), res(wgate),
                  res(qn), res(kvn), res(wuq), res(wuk), res(wuv),
                  pl.BlockSpec((tm, LANES), pos), pl.BlockSpec((tm, LANES), pos),
                  pl.BlockSpec((tm, LANES), pos)],
        out_specs=[pl.BlockSpec((tm, w), row) for w in widths],
        out_shape=[jax.ShapeDtypeStruct((n, w), BF16) for w in widths],
        compiler_params=_params(1),
        name="proj",
    )(h, gain, wlat, wb, wc, wgate, qn, kvn, wuq, wuk, wuv, cos_t, sin_lo, sin_hi)


def _low_half(shape):
    return lax.broadcasted_iota(jnp.int32, shape, len(shape) - 1) < LANES // 2


def _half(x, low, t):
    return jnp.where(low if t == 0 else jnp.logical_not(low), x, jnp.zeros_like(x))


def _with_ones(vx_ref, v_ref):
    vx_ref[:, :LANES] = v_ref[...]
    vx_ref[:, LANES:] = jnp.ones((v_ref.shape[0], LANES), BF16)


def _mla_body(q_ref, k_ref, v_ref, o_ref, vx_ref):
    _with_ones(vx_ref, v_ref)
    vx = vx_ref[...]
    tq = MLA_Q_ROWS
    for qi in range(q_ref.shape[0] // tq):
        rows = slice(qi * tq, (qi + 1) * tq)
        outs = []
        for j in range(2):
            sl = slice(j * LANES, (j + 1) * LANES)
            s = _dot_nt(q_ref[rows, sl], k_ref[:, sl])
            p = jnp.exp2(s - jnp.max(s, axis=-1, keepdims=True)).astype(BF16)
            ol = _dot(p, vx)
            outs.append(ol[:, :LANES] / ol[:, LANES:])
        o_ref[rows, :] = jnp.where(_low_half(outs[0].shape), outs[0], outs[1]).astype(BF16)


def _mla(qa, ka, va):
    b, s, _ = va.shape
    tr = MLA_STEP_ROWS
    return pl.pallas_call(
        _mla_body,
        grid=(b, MLA_HEADS // 2, s // tr),
        in_specs=[pl.BlockSpec((None, tr, 2 * LANES), lambda bi, hp, qi: (bi, qi, hp)),
                  pl.BlockSpec((None, s, 2 * LANES), lambda bi, hp, qi: (bi, 0, hp)),
                  pl.BlockSpec((None, s, LANES), lambda bi, hp, qi: (bi, 0, hp))],
        out_specs=pl.BlockSpec((None, tr, LANES), lambda bi, hp, qi: (bi, qi, hp)),
        out_shape=jax.ShapeDtypeStruct(va.shape, BF16),
        scratch_shapes=[pltpu.VMEM((s, 2 * LANES), BF16)],
        compiler_params=_params(3),
        name="mla",
    )(qa, ka, va)


def _swa_body(sink_ref, q_ref, k_ref, v_ref, bias_ref, o_ref, vx_ref):
    group = pl.program_id(1)
    per_kv = SWA_HEADS // SWA_KV_HEADS
    qb = Q_BLOCK
    n_blocks = q_ref.shape[0] // qb
    _with_ones(vx_ref, v_ref)
    low = _low_half((qb, LANES))
    row = lax.broadcasted_iota(jnp.int32, (per_kv * qb, 1), 0)
    sink = jnp.full((per_kv * qb, 1), sink_ref[per_kv * group + per_kv - 1], F32)
    for t in range(per_kv - 2, -1, -1):
        sink = jnp.where(row < (t + 1) * qb, sink_ref[per_kv * group + t], sink)
    for j in range(n_blocks):
        rows = slice(j * qb, (j + 1) * qb)
        qa, qc = q_ref[rows, :LANES], q_ref[rows, LANES:]
        lhs = jnp.concatenate([_half(qa, low, 0), _half(qa, low, 1),
                               _half(qc, low, 0), _half(qc, low, 1)], axis=0)
        lo = max(j - 1, 0) * qb
        hi = min(j + 2, n_blocks) * qb
        c0 = lo - (j - 1) * qb
        s = _dot_nt(lhs, k_ref[lo:hi, :]) + bias_ref[:, c0:c0 + (hi - lo)]
        m = jnp.maximum(jnp.max(s, axis=-1, keepdims=True), sink)
        p = jnp.exp2(s - m).astype(BF16)
        ol = _dot(p, vx_ref[lo:hi, :])
        o = ol[:, :LANES] / (ol[:, LANES:LANES + 1] + jnp.exp2(sink - m))
        o_ref[rows, :LANES] = jnp.where(low, o[:qb], o[qb:2 * qb]).astype(BF16)
        o_ref[rows, LANES:] = jnp.where(low, o[2 * qb:3 * qb], o[3 * qb:]).astype(BF16)


def _swa(sink, qb, kb, vb, bias):
    b, s, _ = qb.shape
    per_kv = SWA_HEADS // SWA_KV_HEADS
    return pl.pallas_call(
        _swa_body,
        grid=(b, SWA_KV_HEADS),
        in_specs=[pl.BlockSpec(memory_space=pltpu.SMEM),
                  pl.BlockSpec((None, s, 2 * LANES), lambda bi, g: (bi, 0, g)),
                  pl.BlockSpec((None, s, LANES), lambda bi, g: (bi, 0, g)),
                  pl.BlockSpec((None, s, LANES), lambda bi, g: (bi, 0, g)),
                  pl.BlockSpec((None, per_kv * Q_BLOCK, 3 * Q_BLOCK), lambda bi, g: (g, 0, 0))],
        out_specs=pl.BlockSpec((None, s, 2 * LANES), lambda bi, g: (bi, 0, g)),
        out_shape=jax.ShapeDtypeStruct(qb.shape, BF16),
        scratch_shapes=[pltpu.VMEM((s, 2 * LANES), BF16)],
        compiler_params=_params(2),
        name="swa",
    )(sink, qb, kb, vb, bias)


def _diff_tiles(first_tile, n_tiles, seq, lam, c_left, c_right,
                q_ref, k_ref, v_ref, near_ref, subln_ref, o_ref, lambda_init):
    tq = DIFF_Q_ROWS
    low = _low_half((tq, LANES))
    k = k_ref[...]
    v = v_ref[...]
    for i in range(n_tiles):
        qt = first_tile + i
        rows = slice(i * tq, (i + 1) * tq)
        q = q_ref[rows, :]
        a = max(0, qt - 1) * tq
        b = min(seq // tq, qt + 2) * tq
        t0 = a - (qt - 1) * tq
        maps = []
        for t in range(2):
            s = _dot_nt(_half(q, low, t), k)
            near = s[:, a:b] + near_ref[:, t0:t0 + (b - a)]
            m = jnp.max(near, axis=-1, keepdims=True)
            if a > 0:
                m = jnp.maximum(m, jnp.max(s[:, :a], axis=-1, keepdims=True) + c_left)
            if b < seq:
                m = jnp.maximum(m, jnp.max(s[:, b:], axis=-1, keepdims=True) + c_right)
            parts = []
            if a > 0:
                parts.append(jnp.exp2(s[:, :a] - (m - c_left)))
            parts.append(jnp.exp2(near - m))
            if b < seq:
                parts.append(jnp.exp2(s[:, b:] - (m - c_right)))
            l = sum(jnp.sum(part, axis=-1, keepdims=True) for part in parts)
            maps.append((parts, l))
        (p0, l0), (p1, l1) = maps
        c = lam * l0 / l1
        attn = jnp.concatenate([(x - c * y).astype(BF16) for x, y in zip(p0, p1)], axis=1)
        o = _dot(attn, v) / l0
        o_ref[rows, :] = (_rms(o, subln_ref[...]) * (1.0 - lambda_init)).astype(BF16)


def _diff_body(lam_ref, far_ref, q_ref, k_ref, v_ref, near_ref, subln_ref, o_ref, *, lambda_init):
    head = pl.program_id(0)
    seq = k_ref.shape[0]
    n_tiles = q_ref.shape[0] // DIFF_Q_ROWS
    lv = lam_ref[...]
    lam = (jnp.exp(jnp.sum(lv[0:1] * lv[1:2], axis=-1, keepdims=True))
           - jnp.exp(jnp.sum(lv[2:3] * lv[3:4], axis=-1, keepdims=True)) + lambda_init)
    c_left, c_right = far_ref[2 * head], far_ref[2 * head + 1]
    for part in range(seq // q_ref.shape[0]):
        @pl.when(pl.program_id(2) == part)
        def _():
            _diff_tiles(part * n_tiles, n_tiles, seq, lam, c_left, c_right,
                        q_ref, k_ref, v_ref, near_ref, subln_ref, o_ref, lambda_init)


def _diff(layer, lambda_init, lam_vecs, far, qc, kc, vc, near, subln):
    b, s, _ = qc.shape
    tr = DIFF_STEP_ROWS
    return pl.pallas_call(
        functools.partial(_diff_body, lambda_init=lambda_init),
        grid=(DIFF_HEADS, b, s // tr),
        in_specs=[pl.BlockSpec((None, 4, DIFF_HEAD_DIM), lambda hd, bi, qi: (layer, 0, 0)),
                  pl.BlockSpec(memory_space=pltpu.SMEM),
                  pl.BlockSpec((None, tr, LANES), lambda hd, bi, qi: (bi, qi, hd)),
                  pl.BlockSpec((None, s, LANES), lambda hd, bi, qi: (bi, 0, hd)),
                  pl.BlockSpec((None, s, LANES), lambda hd, bi, qi: (bi, 0, hd)),
                  pl.BlockSpec((None, DIFF_Q_ROWS, 3 * DIFF_Q_ROWS), lambda hd, bi, qi: (hd, 0, 0)),
                  pl.BlockSpec((None, 1, LANES), lambda hd, bi, qi: (layer, 0, 0))],
        out_specs=pl.BlockSpec((None, tr, LANES), lambda hd, bi, qi: (bi, qi, hd)),
        out_shape=jax.ShapeDtypeStruct(qc.shape, BF16),
        compiler_params=_params(3),
        name="diff",
    )(lam_vecs, far, qc, kc, vc, near, subln)


def _merge_body(h_ref, sg_ref, oa_ref, ob_ref, oc_ref, w_ref, o_ref):
    d = D_MODEL
    merged = (sg_ref[:, :d].astype(F32) * oa_ref[...].astype(F32)
              + sg_ref[:, d:2 * d].astype(F32) * ob_ref[...].astype(F32)
              + sg_ref[:, 2 * d:].astype(F32) * oc_ref[...].astype(F32))
    o_ref[...] = h_ref[...] + _dot(merged.astype(BF16), w_ref[...])


def _merge(h, layer, sg, oa, ob, oc, w_out):
    n = h.shape[0]
    tm = MERGE_ROWS
    row = lambda r: (r, 0)
    return pl.pallas_call(
        _merge_body,
        grid=(n // tm,),
        in_specs=[pl.BlockSpec((tm, D_MODEL), row), pl.BlockSpec((tm, 3 * D_MODEL), row),
                  pl.BlockSpec((tm, D_MODEL), row), pl.BlockSpec((tm, D_MODEL), row),
                  pl.BlockSpec((tm, D_MODEL), row),
                  _resident((None, D_MODEL, D_MODEL), lambda r: (layer, 0, 0))],
        out_specs=pl.BlockSpec((tm, D_MODEL), row),
        out_shape=jax.ShapeDtypeStruct(h.shape, F32),
        compiler_params=_params(1),
        name="merge",
    )(h, sg, oa, ob, oc, w_out)


def _ple_body(h_ref, p_ref, gain_ref, wg_ref, wp_ref, fin_ref, o_ref, *, final):
    h = h_ref[...]
    gate = jax.nn.sigmoid(_dot(_rms(h, gain_ref[...]).astype(BF16), wg_ref[...]))
    h = h + gate * _dot(p_ref[...].astype(BF16), wp_ref[...])
    o_ref[...] = _rms(h, fin_ref[...]) if final else h


def _ple(h, p, layer, gain, wg, wp, final_norm, final):
    n = h.shape[0]
    tm = PLE_ROWS
    row = lambda r: (r, 0)
    lay3 = lambda r: (layer, 0, 0)
    return pl.pallas_call(
        functools.partial(_ple_body, final=final),
        grid=(n // tm,),
        in_specs=[pl.BlockSpec((tm, D_MODEL), row),
                  pl.BlockSpec((None, tm, PLE_DIM), lambda r: (layer, r, 0)),
                  _resident((None, 1, D_MODEL), lay3),
                  _resident((None, D_MODEL, D_MODEL), lay3),
                  _resident((None, PLE_DIM, D_MODEL), lay3),
                  _resident((1, D_MODEL), lambda r: (0, 0))],
        out_specs=pl.BlockSpec((tm, D_MODEL), row),
        out_shape=jax.ShapeDtypeStruct(h.shape, F32),
        compiler_params=_params(1),
        name="ple",
    )(h, p, gain, wg, wp, final_norm)


def _bias_body(table_ref, idx_ref, o_ref, *, first_head, n_heads):
    head = first_head + pl.program_id(0)
    idx = idx_ref[...]
    acc = jnp.full(idx.shape, NEG_INF, F32)
    for bucket in range(REL_BUCKETS):
        acc = jnp.where(idx == bucket, table_ref[bucket * n_heads + head] * LOG2E, acc)
    o_ref[...] = acc


def _bias_tiles(table_flat, idx, first_head, count, n_heads):
    return pl.pallas_call(
        functools.partial(_bias_body, first_head=first_head, n_heads=n_heads),
        grid=(count,),
        in_specs=[pl.BlockSpec(memory_space=pltpu.SMEM),
                  _resident(idx.shape, lambda hd: (0, 0))],
        out_specs=pl.BlockSpec((None,) + idx.shape, lambda hd: (hd, 0, 0)),
        out_shape=jax.ShapeDtypeStruct((count,) + idx.shape, F32),
        compiler_params=_params(1),
        name="bias",
    )(table_flat, idx)


def _t5_bucket_np(rel):
    nb = REL_BUCKETS // 2
    max_exact = nb // 2
    base = np.where(rel > 0, nb, 0)
    n = np.abs(rel)
    nf = np.maximum(n, 1).astype(np.float64)
    large = max_exact + (np.log(nf / max_exact) / math.log(REL_MAX_DIST / max_exact)
                         * (nb - max_exact)).astype(np.int64)
    large = np.minimum(large, nb - 1)
    return (base + np.where(n < max_exact, n, large)).astype(np.int32)


def _bias_indices():
    rel = (np.arange(3 * Q_BLOCK)[None, :] - WINDOW) - np.arange(Q_BLOCK)[:, None]
    idx_b = np.where(np.abs(rel) <= WINDOW, _t5_bucket_np(rel), REL_BUCKETS).astype(np.int32)
    t = DIFF_Q_ROWS
    rel = (np.arange(3 * t)[None, :] - t) - np.arange(t)[:, None]
    idx_c = _t5_bucket_np(rel)
    far = _t5_bucket_np(np.array([-t, t]))
    assert (far == _t5_bucket_np(np.array([-REL_MAX_DIST, REL_MAX_DIST]))).all()
    return idx_b, idx_c, far


def _rope_tables(seq):
    half = MLA_ROPE // 2
    inv = ROPE_THETA ** (-jnp.arange(0, MLA_ROPE, 2, dtype=F32) / MLA_ROPE)
    ang = jnp.arange(seq, dtype=F32)[:, None] * inv[None, :]
    cos, sin = jnp.cos(ang), jnp.sin(ang)
    z = lambda w: jnp.zeros((seq, w), F32)
    cos_t = jnp.concatenate([z(MLA_NOPE), cos, cos, z(LANES - MLA_NOPE - MLA_ROPE)], axis=1)
    sin_lo = jnp.concatenate([z(MLA_NOPE), -sin, z(LANES - MLA_NOPE - half)], axis=1)
    sin_hi = jnp.concatenate([z(MLA_NOPE + half), sin, z(LANES - MLA_NOPE - MLA_ROPE)], axis=1)
    return cos_t, sin_lo, sin_hi


def _dup_heads(w, heads, dim):
    w4 = w.reshape(w.shape[:2] + (heads, dim))
    return jnp.concatenate([w4, w4], axis=-1).reshape(w.shape[:2] + (heads * 2 * dim,))


def kernel(x, p, ffn1_norm, ffn1_w_gate, ffn1_w_up, ffn1_w_down, mix_norm, w_in, mla_q_norm, mla_w_uq, mla_kv_norm, mla_w_ukv, swa_sink, diff_lambda_q1, diff_lambda_k1, diff_lambda_q2, diff_lambda_k2, diff_subln, rel_table, w_out, ffn2_norm, ffn2_w_gate, ffn2_w_up, ffn2_w_down, ple_norm, ple_w_gate, ple_w_proj, final_norm):
    b, s, d = x.shape
    depth = w_in.shape[0]
    n = b * s
    bf = lambda a: a.astype(BF16)
    vec = lambda a: a.astype(F32)[:, None, :]

    o = np.cumsum((0, MLA_Q_LORA, MLA_KV_LORA, MLA_ROPE, 1024, 256, 256, 1024, 1024, 1024, 3072))
    col = lambda i: w_in[:, :, o[i]:o[i + 1]]
    zeros = lambda w: jnp.zeros((depth, d, w), w_in.dtype)
    wlat = bf(jnp.concatenate([col(0), col(1), zeros(MLA_NOPE), col(2),
                               zeros(LANES - MLA_NOPE - MLA_ROPE)], axis=-1))
    wb = bf(jnp.concatenate([col(3), _dup_heads(col(4), SWA_KV_HEADS, SWA_HEAD_DIM),
                             _dup_heads(col(5), SWA_KV_HEADS, SWA_HEAD_DIM)], axis=-1))
    wc = bf(jnp.concatenate([col(6), col(7), col(8)], axis=-1))
    wgate = bf(col(9))
    uq = mla_w_uq.reshape(depth, MLA_Q_LORA, MLA_HEADS, MLA_NOPE + MLA_ROPE)
    wuq = bf(jnp.pad(uq, ((0, 0),) * 3 + ((0, LANES - MLA_NOPE - MLA_ROPE),))
             .reshape(depth, MLA_Q_LORA, MLA_HEADS * LANES))
    ukv = mla_w_ukv.reshape(depth, MLA_KV_LORA, MLA_HEADS, MLA_NOPE + MLA_V)
    wuk = bf(jnp.pad(ukv[..., :MLA_NOPE], ((0, 0),) * 3 + ((0, LANES - MLA_NOPE),))
             .reshape(depth, MLA_KV_LORA, MLA_HEADS * LANES))
    wuv = bf(ukv[..., MLA_NOPE:].reshape(depth, MLA_KV_LORA, MLA_HEADS * MLA_V))

    cos_t, sin_lo, sin_hi = _rope_tables(s)
    idx_b, idx_c, far_buckets = _bias_indices()
    n_bias_heads = SWA_HEADS + DIFF_HEADS
    table_flat = rel_table.astype(F32).reshape(REL_BUCKETS * n_bias_heads)
    per_kv = SWA_HEADS // SWA_KV_HEADS
    bias_b = _bias_tiles(table_flat, jnp.asarray(idx_b), 0, SWA_HEADS, n_bias_heads).reshape(
        SWA_KV_HEADS, per_kv * Q_BLOCK, 3 * Q_BLOCK)
    near_c = _bias_tiles(table_flat, jnp.asarray(idx_c), SWA_HEADS, DIFF_HEADS, n_bias_heads)
    far_c = (jnp.stack([rel_table[int(far_buckets[0]), SWA_HEADS:],
                        rel_table[int(far_buckets[1]), SWA_HEADS:]], axis=1).astype(F32)
             * LOG2E).reshape(2 * DIFF_HEADS)
    sink = swa_sink.astype(F32) * LOG2E
    lam_vecs = jnp.stack([diff_lambda_q1, diff_lambda_k1, diff_lambda_q2, diff_lambda_k2],
                         axis=1).astype(F32)
    f1 = (vec(ffn1_norm), bf(ffn1_w_gate), bf(ffn1_w_up), bf(ffn1_w_down))
    f2 = (vec(ffn2_norm), bf(ffn2_w_gate), bf(ffn2_w_up), bf(ffn2_w_down))
    mixn, qn, kvn, subln, plen = (vec(mix_norm), vec(mla_q_norm), vec(mla_kv_norm),
                                  vec(diff_subln), vec(ple_norm))
    wo, wpg, wpp = bf(w_out), bf(ple_w_gate), bf(ple_w_proj)
    fin = final_norm.astype(F32)[None, :]
    p2 = p.reshape(depth, n, PLE_DIM)

    h = x.reshape(n, d).astype(F32)
    for i in range(depth):
        h = _ffn(h, i, *f1)
        qa, ka, va, qb, kb, vb, qc, kc, vc, sg = _proj(
            h, i, s, mixn, wlat, wb, wc, wgate, qn, kvn, wuq, wuk, wuv, cos_t, sin_lo, sin_hi)
        r3 = lambda a: a.reshape(b, s, a.shape[-1])
        oa = _mla(r3(qa), r3(ka), r3(va))
        ob = _swa(sink[i], r3(qb), r3(kb), r3(vb), bias_b)
        lambda_init = 0.8 - 0.6 * math.exp(-0.3 * i)
        oc = _diff(i, lambda_init, lam_vecs, far_c, r3(qc), r3(kc), r3(vc), near_c, subln)
        h = _merge(h, i, sg, oa.reshape(n, d), ob.reshape(n, d), oc.reshape(n, d), wo)
        h = _ffn(h, i, *f2)
        h = _ple(h, p2, i, plen, wpg, wpp, fin, final=(i == depth - 1))
    return h.reshape(b, s, d).astype(x.dtype)
```

```python
import functools
import math

import jax
import jax.numpy as jnp
import numpy as np
from jax import lax
from jax.experimental import pallas as pl
from jax.experimental.pallas import tpu as pltpu

F32 = jnp.float32
BF16 = jnp.bfloat16

D_MODEL = 1024
D_FF = 2816
PLE_DIM = 256
NORM_EPS = 1e-6
NEG_INF = -1e30
ROPE_THETA = 10000.0
MLA_HEADS = 16
MLA_NOPE = 64
MLA_ROPE = 32
MLA_V = 64
MLA_Q_LORA = 384
MLA_KV_LORA = 256
SWA_HEADS = 16
SWA_KV_HEADS = 4
SWA_HEAD_DIM = 64
WINDOW = 128
Q_BLOCK = 128
DIFF_HEADS = 8
DIFF_HEAD_DIM = 64
REL_BUCKETS = 32
REL_MAX_DIST = 128
LOG2E = math.log2(math.e)

LANES = 128
VMEM_LIMIT = 56 * 1024 * 1024

FFN_ROWS = 512
PROJ_ROWS = 256
POST_ROWS = 512
MLA_Q_ROWS = 128
DIFF_Q_ROWS = 128


def _params(n_axes):
    return pltpu.CompilerParams(dimension_semantics=("arbitrary",) * n_axes,
                                vmem_limit_bytes=VMEM_LIMIT)


def _resident(shape, index_map):
    return pl.BlockSpec(shape, index_map, pipeline_mode=pl.Buffered(1))


def _rms(x, gain):
    return x * lax.rsqrt(jnp.mean(x * x, axis=-1, keepdims=True) + NORM_EPS) * gain


def _dot(a, b):
    return jnp.dot(a, b, preferred_element_type=F32)


def _dot_nt(a, b):
    return lax.dot_general(a, b, (((1,), (1,)), ((), ())), preferred_element_type=F32)


def _swiglu_half_step(h, gain_ref, wg_ref, wu_ref, wd_ref):
    n = _rms(h, gain_ref[...]).astype(BF16)
    g = _dot(n, wg_ref[...])
    u = _dot(n, wu_ref[...])
    a = (g * jax.nn.sigmoid(g) * u).astype(BF16)
    return h + 0.5 * _dot(a, wd_ref[...])


def _ffn_body(h_ref, gain_ref, wg_ref, wu_ref, wd_ref, o_ref):
    o_ref[...] = _swiglu_half_step(h_ref[...], gain_ref, wg_ref, wu_ref, wd_ref)


def _ffn(h, layer, gain, wg, wu, wd):
    n = h.shape[0]
    tm = FFN_ROWS
    row = lambda r: (r, 0)
    lay3 = lambda r: (layer, 0, 0)
    return pl.pallas_call(
        _ffn_body,
        grid=(n // tm,),
        in_specs=[pl.BlockSpec((tm, D_MODEL), row),
                  _resident((None, 1, D_MODEL), lay3),
                  _resident((None, D_MODEL, D_FF), lay3),
                  _resident((None, D_MODEL, D_FF), lay3),
                  _resident((None, D_FF, D_MODEL), lay3)],
        out_specs=pl.BlockSpec((tm, D_MODEL), row),
        out_shape=jax.ShapeDtypeStruct(h.shape, F32),
        compiler_params=_params(1),
        name="ffn",
    )(h, gain, wg, wu, wd)


def _rope(x, cos_t, sin_lo, sin_hi):
    return (x * cos_t + pltpu.roll(x, LANES - MLA_ROPE // 2, 1) * sin_lo
            + pltpu.roll(x, MLA_ROPE // 2, 1) * sin_hi)


def _proj_body(h_ref, gain_ref, wlat_ref, wb_ref, wc_ref, wgate_ref, qn_ref, kvn_ref,
               wuq_ref, wuk_ref, wuv_ref, cos_ref, slo_ref, shi_ref,
               qa_ref, ka_ref, va_ref, qb_ref, kb_ref, vb_ref, qc_ref, kc_ref, vc_ref, sg_ref):
    u = _rms(h_ref[...], gain_ref[...]).astype(BF16)

    lat = _dot(u, wlat_ref[...])
    cq = _rms(lat[:, :MLA_Q_LORA], qn_ref[...]).astype(BF16)
    ckv = _rms(lat[:, MLA_Q_LORA:MLA_Q_LORA + MLA_KV_LORA], kvn_ref[...]).astype(BF16)
    kr = lat[:, MLA_Q_LORA + MLA_KV_LORA:]

    cos_t, sin_lo, sin_hi = cos_ref[...], slo_ref[...], shi_ref[...]
    lane = lax.broadcasted_iota(jnp.int32, cos_t.shape, 1)
    scale = (MLA_NOPE + MLA_ROPE) ** -0.5 * LOG2E
    q_cos = (cos_t + jnp.where(lane < MLA_NOPE, 1.0, 0.0)) * scale
    q_lo, q_hi = sin_lo * scale, sin_hi * scale
    kr_rot = _rope(kr, cos_t, sin_lo, sin_hi)

    q = _dot(cq, wuq_ref[...])
    kn = _dot(ckv, wuk_ref[...])
    for hd in range(MLA_HEADS):
        sl = slice(hd * LANES, (hd + 1) * LANES)
        qa_ref[:, sl] = _rope(q[:, sl], q_cos, q_lo, q_hi).astype(BF16)
        ka_ref[:, sl] = (kn[:, sl] + kr_rot).astype(BF16)
    va_ref[...] = _dot(ckv, wuv_ref[...]).astype(BF16)

    pb = _dot(u, wb_ref[...])
    qb_ref[...] = (pb[:, :1024] * (SWA_HEAD_DIM ** -0.5 * LOG2E)).astype(BF16)
    kb_ref[...] = pb[:, 1024:1536].astype(BF16)
    vb_ref[...] = pb[:, 1536:].astype(BF16)
    pc = _dot(u, wc_ref[...])
    qc_ref[...] = (pc[:, :1024] * (DIFF_HEAD_DIM ** -0.5 * LOG2E)).astype(BF16)
    kc_ref[...] = pc[:, 1024:2048].astype(BF16)
    vc_ref[...] = pc[:, 2048:].astype(BF16)
    sg_ref[...] = jax.nn.sigmoid(_dot(u, wgate_ref[...])).astype(BF16)


def _proj(h, layer, seq, gain, wlat, wb, wc, wgate, qn, kvn, wuq, wuk, wuv, cos_t, sin_lo, sin_hi):
    n = h.shape[0]
    tm = PROJ_ROWS
    row = lambda r: (r, 0)
    lay3 = lambda r: (layer, 0, 0)
    pos = lambda r: (r % (seq // tm), 0)
    widths = (2048, 2048, 1024, 1024, 512, 512, 1024, 1024, 1024, 3072)
    res = lambda a: _resident((None,) + a.shape[1:], lay3)
    return pl.pallas_call(
        _proj_body,
        grid=(n // tm,),
        in_specs=[pl.BlockSpec((tm, D_MODEL), row), res(gain), res(wlat), res(wb), res(wc), res(wgate),
                  res(qn), res(kvn), res(wuq), res(wuk), res(wuv),
                  pl.BlockSpec((tm, LANES), pos), pl.BlockSpec((tm, LANES), pos),
                  pl.BlockSpec((tm, LANES), pos)],
        out_specs=[pl.BlockSpec((tm, w), row) for w in widths],
        out_shape=[jax.ShapeDtypeStruct((n, w), BF16) for w in widths],
        compiler_params=_params(1),
        name="proj",
    )(h, gain, wlat, wb, wc, wgate, qn, kvn, wuq, wuk, wuv, cos_t, sin_lo, sin_hi)


def _low_half(shape):
    return lax.broadcasted_iota(jnp.int32, shape, len(shape) - 1) < LANES // 2


def _half(x, low, t):
    return jnp.where(low if t == 0 else jnp.logical_not(low), x, jnp.zeros_like(x))


def _with_ones(vx_ref, v_ref):
    vx_ref[:, :LANES] = v_ref[...]
    vx_ref[:, LANES:] = jnp.ones((v_ref.shape[0], LANES), BF16)


def _mla_body(q_ref, k_ref, v_ref, o_ref, vx_ref):
    _with_ones(vx_ref, v_ref)
    vx = vx_ref[...]
    tq = MLA_Q_ROWS
    for qi in range(q_ref.shape[0] // tq):
        rows = slice(qi * tq, (qi + 1) * tq)
        outs = []
        for j in range(2):
            sl = slice(j * LANES, (j + 1) * LANES)
            s = _dot_nt(q_ref[rows, sl], k_ref[:, sl])
            p = jnp.exp2(s - jnp.max(s, axis=-1, keepdims=True)).astype(BF16)
            ol = _dot(p, vx)
            outs.append(ol[:, :LANES] / ol[:, LANES:])
        o_ref[rows, :] = jnp.where(_low_half(outs[0].shape), outs[0], outs[1]).astype(BF16)


def _mla(qa, ka, va):
    b, s, _ = va.shape
    pair = lambda bi, hp: (bi, 0, hp)
    return pl.pallas_call(
        _mla_body,
        grid=(b, MLA_HEADS // 2),
        in_specs=[pl.BlockSpec((None, s, 2 * LANES), pair),
                  pl.BlockSpec((None, s, 2 * LANES), pair),
                  pl.BlockSpec((None, s, LANES), pair)],
        out_specs=pl.BlockSpec((None, s, LANES), pair),
        out_shape=jax.ShapeDtypeStruct(va.shape, BF16),
        scratch_shapes=[pltpu.VMEM((s, 2 * LANES), BF16)],
        compiler_params=_params(2),
        name="mla",
    )(qa, ka, va)


def _swa_body(sink_ref, q_ref, k_ref, v_ref, bias_ref, o_ref, vx_ref):
    group = pl.program_id(1)
    per_kv = SWA_HEADS // SWA_KV_HEADS
    qb = Q_BLOCK
    n_blocks = q_ref.shape[0] // qb
    _with_ones(vx_ref, v_ref)
    low = _low_half((qb, LANES))
    row = lax.broadcasted_iota(jnp.int32, (per_kv * qb, 1), 0)
    sink = jnp.full((per_kv * qb, 1), sink_ref[per_kv * group + per_kv - 1], F32)
    for t in range(per_kv - 2, -1, -1):
        sink = jnp.where(row < (t + 1) * qb, sink_ref[per_kv * group + t], sink)
    for j in range(n_blocks):
        rows = slice(j * qb, (j + 1) * qb)
        qa, qc = q_ref[rows, :LANES], q_ref[rows, LANES:]
        lhs = jnp.concatenate([_half(qa, low, 0), _half(qa, low, 1),
                               _half(qc, low, 0), _half(qc, low, 1)], axis=0)
        lo = max(j - 1, 0) * qb
        hi = min(j + 2, n_blocks) * qb
        c0 = lo - (j - 1) * qb
        s = _dot_nt(lhs, k_ref[lo:hi, :]) + bias_ref[:, c0:c0 + (hi - lo)]
        m = jnp.maximum(jnp.max(s, axis=-1, keepdims=True), sink)
        p = jnp.exp2(s - m).astype(BF16)
        ol = _dot(p, vx_ref[lo:hi, :])
        o = ol[:, :LANES] / (ol[:, LANES:LANES + 1] + jnp.exp2(sink - m))
        o_ref[rows, :LANES] = jnp.where(low, o[:qb], o[qb:2 * qb]).astype(BF16)
        o_ref[rows, LANES:] = jnp.where(low, o[2 * qb:3 * qb], o[3 * qb:]).astype(BF16)


def _swa(sink, qb, kb, vb, bias):
    b, s, _ = qb.shape
    per_kv = SWA_HEADS // SWA_KV_HEADS
    return pl.pallas_call(
        _swa_body,
        grid=(b, SWA_KV_HEADS),
        in_specs=[pl.BlockSpec(memory_space=pltpu.SMEM),
                  pl.BlockSpec((None, s, 2 * LANES), lambda bi, g: (bi, 0, g)),
                  pl.BlockSpec((None, s, LANES), lambda bi, g: (bi, 0, g)),
                  pl.BlockSpec((None, s, LANES), lambda bi, g: (bi, 0, g)),
                  pl.BlockSpec((None, per_kv * Q_BLOCK, 3 * Q_BLOCK), lambda bi, g: (g, 0, 0))],
        out_specs=pl.BlockSpec((None, s, 2 * LANES), lambda bi, g: (bi, 0, g)),
        out_shape=jax.ShapeDtypeStruct(qb.shape, BF16),
        scratch_shapes=[pltpu.VMEM((s, 2 * LANES), BF16)],
        compiler_params=_params(2),
        name="swa",
    )(sink, qb, kb, vb, bias)


def _diff_body(lam_ref, far_ref, q_ref, k_ref, v_ref, near_ref, subln_ref, o_ref, vx_ref, *, lambda_init):
    head = pl.program_id(0)
    seq = k_ref.shape[0]
    tq = DIFF_Q_ROWS
    lv = lam_ref[...]
    lam = (jnp.exp(jnp.sum(lv[0:1] * lv[1:2], axis=-1, keepdims=True))
           - jnp.exp(jnp.sum(lv[2:3] * lv[3:4], axis=-1, keepdims=True)) + lambda_init)
    c_left, c_right = far_ref[2 * head], far_ref[2 * head + 1]
    low = _low_half((tq, LANES))
    k = k_ref[...]
    _with_ones(vx_ref, v_ref)
    vx = vx_ref[...]
    for qt in range(seq // tq):
        rows = slice(qt * tq, (qt + 1) * tq)
        q = q_ref[rows, :]
        a = max(0, qt - 1) * tq
        b = min(seq // tq, qt + 2) * tq
        t0 = a - (qt - 1) * tq
        maps = []
        for t in range(2):
            s = _dot_nt(_half(q, low, t), k)
            near = s[:, a:b] + near_ref[:, t0:t0 + (b - a)]
            m = jnp.max(near, axis=-1, keepdims=True)
            if a > 0:
                m = jnp.maximum(m, jnp.max(s[:, :a], axis=-1, keepdims=True) + c_left)
            if b < seq:
                m = jnp.maximum(m, jnp.max(s[:, b:], axis=-1, keepdims=True) + c_right)
            parts = []
            if a > 0:
                parts.append(jnp.exp2(s[:, :a] - (m - c_left)).astype(BF16))
            parts.append(jnp.exp2(near - m).astype(BF16))
            if b < seq:
                parts.append(jnp.exp2(s[:, b:] - (m - c_right)).astype(BF16))
            ol = _dot(jnp.concatenate(parts, axis=1), vx)
            maps.append(ol[:, :LANES] / ol[:, LANES:])
        o = maps[0] - lam * maps[1]
        o_ref[rows, :] = (_rms(o, subln_ref[...]) * (1.0 - lambda_init)).astype(BF16)


def _diff(layer, lambda_init, lam_vecs, far, qc, kc, vc, near, subln):
    b, s, _ = qc.shape
    head_block = lambda hd, bi: (bi, 0, hd)
    return pl.pallas_call(
        functools.partial(_diff_body, lambda_init=lambda_init),
        grid=(DIFF_HEADS, b),
        in_specs=[pl.BlockSpec((None, 4, DIFF_HEAD_DIM), lambda hd, bi: (layer, 0, 0)),
                  pl.BlockSpec(memory_space=pltpu.SMEM),
                  pl.BlockSpec((None, s, LANES), head_block),
                  pl.BlockSpec((None, s, LANES), head_block),
                  pl.BlockSpec((None, s, LANES), head_block),
                  pl.BlockSpec((None, DIFF_Q_ROWS, 3 * DIFF_Q_ROWS), lambda hd, bi: (hd, 0, 0)),
                  pl.BlockSpec((None, 1, LANES), lambda hd, bi: (layer, 0, 0))],
        out_specs=pl.BlockSpec((None, s, LANES), head_block),
        out_shape=jax.ShapeDtypeStruct(qc.shape, BF16),
        scratch_shapes=[pltpu.VMEM((s, 2 * LANES), BF16)],
        compiler_params=_params(2),
        name="diff",
    )(lam_vecs, far, qc, kc, vc, near, subln)


def _post_body(h_ref, sg_ref, oa_ref, ob_ref, oc_ref, wo_ref, gain_ref, wg_ref, wu_ref, wd_ref,
               p_ref, pgain_ref, wpg_ref, wpp_ref, fin_ref, o_ref, *, final):
    d = D_MODEL
    merged = (sg_ref[:, :d].astype(F32) * oa_ref[...].astype(F32)
              + sg_ref[:, d:2 * d].astype(F32) * ob_ref[...].astype(F32)
              + sg_ref[:, 2 * d:].astype(F32) * oc_ref[...].astype(F32))
    h = h_ref[...] + _dot(merged.astype(BF16), wo_ref[...])
    h = _swiglu_half_step(h, gain_ref, wg_ref, wu_ref, wd_ref)
    gate = jax.nn.sigmoid(_dot(_rms(h, pgain_ref[...]).astype(BF16), wpg_ref[...]))
    h = h + gate * _dot(p_ref[...].astype(BF16), wpp_ref[...])
    o_ref[...] = _rms(h, fin_ref[...]) if final else h


def _post(h, layer, sg, oa, ob, oc, w_out, gain, wg, wu, wd, p, pgain, wpg, wpp, final_norm, final):
    n = h.shape[0]
    tm = POST_ROWS
    row = lambda r: (r, 0)
    lay3 = lambda r: (layer, 0, 0)
    res = lambda a: _resident((None,) + a.shape[1:], lay3)
    return pl.pallas_call(
        functools.partial(_post_body, final=final),
        grid=(n // tm,),
        in_specs=[pl.BlockSpec((tm, D_MODEL), row), pl.BlockSpec((tm, 3 * D_MODEL), row),
                  pl.BlockSpec((tm, D_MODEL), row), pl.BlockSpec((tm, D_MODEL), row),
                  pl.BlockSpec((tm, D_MODEL), row),
                  res(w_out), res(gain), res(wg), res(wu), res(wd),
                  pl.BlockSpec((None, tm, PLE_DIM), lambda r: (layer, r, 0)),
                  res(pgain), res(wpg), res(wpp), _resident((1, D_MODEL), lambda r: (0, 0))],
        out_specs=pl.BlockSpec((tm, D_MODEL), row),
        out_shape=jax.ShapeDtypeStruct(h.shape, F32),
        compiler_params=_params(1),
        name="post",
    )(h, sg, oa, ob, oc, w_out, gain, wg, wu, wd, p, pgain, wpg, wpp, final_norm)


def _bias_body(table_ref, idx_ref, o_ref, *, first_head, n_heads):
    head = first_head + pl.program_id(0)
    idx = idx_ref[...]
    acc = jnp.full(idx.shape, NEG_INF, F32)
    for bucket in range(REL_BUCKETS):
        acc = jnp.where(idx == bucket, table_ref[bucket * n_heads + head] * LOG2E, acc)
    o_ref[...] = acc


def _bias_tiles(table_flat, idx, first_head, count, n_heads):
    return pl.pallas_call(
        functools.partial(_bias_body, first_head=first_head, n_heads=n_heads),
        grid=(count,),
        in_specs=[pl.BlockSpec(memory_space=pltpu.SMEM),
                  _resident(idx.shape, lambda hd: (0, 0))],
        out_specs=pl.BlockSpec((None,) + idx.shape, lambda hd: (hd, 0, 0)),
        out_shape=jax.ShapeDtypeStruct((count,) + idx.shape, F32),
        compiler_params=_params(1),
        name="bias",
    )(table_flat, idx)


def _t5_bucket_np(rel):
    nb = REL_BUCKETS // 2
    max_exact = nb // 2
    base = np.where(rel > 0, nb, 0)
    n = np.abs(rel)
    nf = np.maximum(n, 1).astype(np.float64)
    large = max_exact + (np.log(nf / max_exact) / math.log(REL_MAX_DIST / max_exact)
                         * (nb - max_exact)).astype(np.int64)
    large = np.minimum(large, nb - 1)
    return (base + np.where(n < max_exact, n, large)).astype(np.int32)


def _bias_indices():
    rel = (np.arange(3 * Q_BLOCK)[None, :] - WINDOW) - np.arange(Q_BLOCK)[:, None]
    idx_b = np.where(np.abs(rel) <= WINDOW, _t5_bucket_np(rel), REL_BUCKETS).astype(np.int32)
    t = DIFF_Q_ROWS
    rel = (np.arange(3 * t)[None, :] - t) - np.arange(t)[:, None]
    idx_c = _t5_bucket_np(rel)
    far = _t5_bucket_np(np.array([-t, t]))
    assert (far == _t5_bucket_np(np.array([-REL_MAX_DIST, REL_MAX_DIST]))).all()
    return idx_b, idx_c, far


def _rope_tables(seq):
    half = MLA_ROPE // 2
    inv = ROPE_THETA ** (-jnp.arange(0, MLA_ROPE, 2, dtype=F32) / MLA_ROPE)
    ang = jnp.arange(seq, dtype=F32)[:, None] * inv[None, :]
    cos, sin = jnp.cos(ang), jnp.sin(ang)
    z = lambda w: jnp.zeros((seq, w), F32)
    cos_t = jnp.concatenate([z(MLA_NOPE), cos, cos, z(LANES - MLA_NOPE - MLA_ROPE)], axis=1)
    sin_lo = jnp.concatenate([z(MLA_NOPE), -sin, z(LANES - MLA_NOPE - half)], axis=1)
    sin_hi = jnp.concatenate([z(MLA_NOPE + half), sin, z(LANES - MLA_NOPE - MLA_ROPE)], axis=1)
    return cos_t, sin_lo, sin_hi


def _dup_heads(w, heads, dim):
    w4 = w.reshape(w.shape[:2] + (heads, dim))
    return jnp.concatenate([w4, w4], axis=-1).reshape(w.shape[:2] + (heads * 2 * dim,))


def kernel(x, p, ffn1_norm, ffn1_w_gate, ffn1_w_up, ffn1_w_down, mix_norm, w_in, mla_q_norm, mla_w_uq, mla_kv_norm, mla_w_ukv, swa_sink, diff_lambda_q1, diff_lambda_k1, diff_lambda_q2, diff_lambda_k2, diff_subln, rel_table, w_out, ffn2_norm, ffn2_w_gate, ffn2_w_up, ffn2_w_down, ple_norm, ple_w_gate, ple_w_proj, final_norm):
    b, s, d = x.shape
    depth = w_in.shape[0]
    n = b * s
    bf = lambda a: a.astype(BF16)
    vec = lambda a: a.astype(F32)[:, None, :]

    o = np.cumsum((0, MLA_Q_LORA, MLA_KV_LORA, MLA_ROPE, 1024, 256, 256, 1024, 1024, 1024, 3072))
    col = lambda i: w_in[:, :, o[i]:o[i + 1]]
    zeros = lambda w: jnp.zeros((depth, d, w), w_in.dtype)
    wlat = bf(jnp.concatenate([col(0), col(1), zeros(MLA_NOPE), col(2),
                               zeros(LANES - MLA_NOPE - MLA_ROPE)], axis=-1))
    wb = bf(jnp.concatenate([col(3), _dup_heads(col(4), SWA_KV_HEADS, SWA_HEAD_DIM),
                             _dup_heads(col(5), SWA_KV_HEADS, SWA_HEAD_DIM)], axis=-1))
    wc = bf(jnp.concatenate([col(6), col(7), col(8)], axis=-1))
    wgate = bf(col(9))
    uq = mla_w_uq.reshape(depth, MLA_Q_LORA, MLA_HEADS, MLA_NOPE + MLA_ROPE)
    wuq = bf(jnp.pad(uq, ((0, 0),) * 3 + ((0, LANES - MLA_NOPE - MLA_ROPE),))
             .reshape(depth, MLA_Q_LORA, MLA_HEADS * LANES))
    ukv = mla_w_ukv.reshape(depth, MLA_KV_LORA, MLA_HEADS, MLA_NOPE + MLA_V)
    wuk = bf(jnp.pad(ukv[..., :MLA_NOPE], ((0, 0),) * 3 + ((0, LANES - MLA_NOPE),))
             .reshape(depth, MLA_KV_LORA, MLA_HEADS * LANES))
    wuv = bf(ukv[..., MLA_NOPE:].reshape(depth, MLA_KV_LORA, MLA_HEADS * MLA_V))

    cos_t, sin_lo, sin_hi = _rope_tables(s)
    idx_b, idx_c, far_buckets = _bias_indices()
    n_bias_heads = SWA_HEADS + DIFF_HEADS
    table_flat = rel_table.astype(F32).reshape(REL_BUCKETS * n_bias_heads)
    per_kv = SWA_HEADS // SWA_KV_HEADS
    bias_b = _bias_tiles(table_flat, jnp.asarray(idx_b), 0, SWA_HEADS, n_bias_heads).reshape(
        SWA_KV_HEADS, per_kv * Q_BLOCK, 3 * Q_BLOCK)
    near_c = _bias_tiles(table_flat, jnp.asarray(idx_c), SWA_HEADS, DIFF_HEADS, n_bias_heads)
    far_c = (jnp.stack([rel_table[int(far_buckets[0]), SWA_HEADS:],
                        rel_table[int(far_buckets[1]), SWA_HEADS:]], axis=1).astype(F32)
             * LOG2E).reshape(2 * DIFF_HEADS)
    sink = swa_sink.astype(F32) * LOG2E
    lam_vecs = jnp.stack([diff_lambda_q1, diff_lambda_k1, diff_lambda_q2, diff_lambda_k2],
                         axis=1).astype(F32)
    f1 = (vec(ffn1_norm), bf(ffn1_w_gate), bf(ffn1_w_up), bf(ffn1_w_down))
    f2 = (vec(ffn2_norm), bf(ffn2_w_gate), bf(ffn2_w_up), bf(ffn2_w_down))
    mixn, qn, kvn, subln, plen = (vec(mix_norm), vec(mla_q_norm), vec(mla_kv_norm),
                                  vec(diff_subln), vec(ple_norm))
    wo, wpg, wpp = bf(w_out), bf(ple_w_gate), bf(ple_w_proj)
    fin = final_norm.astype(F32)[None, :]
    p2 = p.reshape(depth, n, PLE_DIM)

    h = x.reshape(n, d).astype(F32)
    for i in range(depth):
        h = _ffn(h, i, *f1)
        qa, ka, va, qb, kb, vb, qc, kc, vc, sg = _proj(
            h, i, s, mixn, wlat, wb, wc, wgate, qn, kvn, wuq, wuk, wuv, cos_t, sin_lo, sin_hi)
        r3 = lambda a: a.reshape(b, s, a.shape[-1])
        oa = _mla(r3(qa), r3(ka), r3(va))
        ob = _swa(sink[i], r3(qb), r3(kb), r3(vb), bias_b)
        lambda_init = 0.8 - 0.6 * math.exp(-0.3 * i)
        oc = _diff(i, lambda_init, lam_vecs, far_c, r3(qc), r3(kc), r3(vc), near_c, subln)
        h = _post(h, i, sg, oa.reshape(n, d), ob.reshape(n, d), oc.reshape(n, d), wo, *f2,
                  p2, plen, wpg, wpp, fin, final=(i == depth - 1))
    return h.reshape(b, s, d).astype(x.dtype)
```

```python
import functools
import math

import jax
import jax.numpy as jnp
import numpy as np
from jax import lax
from jax.experimental import pallas as pl
from jax.experimental.pallas import tpu as pltpu

F32 = jnp.float32
BF16 = jnp.bfloat16

D_MODEL = 1024
D_FF = 2816
PLE_DIM = 256
NORM_EPS = 1e-6
NEG_INF = -1e30
ROPE_THETA = 10000.0
MLA_HEADS = 16
MLA_NOPE = 64
MLA_ROPE = 32
MLA_V = 64
MLA_Q_LORA = 384
MLA_KV_LORA = 256
SWA_HEADS = 16
SWA_KV_HEADS = 4
SWA_HEAD_DIM = 64
WINDOW = 128
Q_BLOCK = 128
DIFF_HEADS = 8
DIFF_HEAD_DIM = 64
REL_BUCKETS = 32
REL_MAX_DIST = 128
LOG2E = math.log2(math.e)

LANES = 128
VMEM_LIMIT = 56 * 1024 * 1024

FFN_ROWS = 512
PROJ_ROWS = 256
POST_ROWS = 512
MLA_Q_ROWS = 128
DIFF_Q_ROWS = 128
MLA_STEP_COLS = 5 * LANES
SWA_STEP_COLS = 4 * LANES
DIFF_STEP_COLS = 3 * LANES


def _params(n_axes):
    return pltpu.CompilerParams(dimension_semantics=("arbitrary",) * n_axes,
                                vmem_limit_bytes=VMEM_LIMIT)


def _resident(shape, index_map):
    return pl.BlockSpec(shape, index_map, pipeline_mode=pl.Buffered(1))


def _rms(x, gain):
    return x * lax.rsqrt(jnp.mean(x * x, axis=-1, keepdims=True) + NORM_EPS) * gain


def _dot(a, b):
    return jnp.dot(a, b, preferred_element_type=F32)


def _dot_nt(a, b):
    return lax.dot_general(a, b, (((1,), (1,)), ((), ())), preferred_element_type=F32)


def _swiglu_half_step(h, gain_ref, wg_ref, wu_ref, wd_ref):
    n = _rms(h, gain_ref[...]).astype(BF16)
    g = _dot(n, wg_ref[...])
    u = _dot(n, wu_ref[...])
    a = (g * jax.nn.sigmoid(g) * u).astype(BF16)
    return h + 0.5 * _dot(a, wd_ref[...])


def _ffn_body(h_ref, gain_ref, wg_ref, wu_ref, wd_ref, o_ref):
    o_ref[...] = _swiglu_half_step(h_ref[...], gain_ref, wg_ref, wu_ref, wd_ref)


def _ffn(h, layer, gain, wg, wu, wd):
    n = h.shape[0]
    tm = FFN_ROWS
    row = lambda r: (r, 0)
    lay3 = lambda r: (layer, 0, 0)
    return pl.pallas_call(
        _ffn_body,
        grid=(n // tm,),
        in_specs=[pl.BlockSpec((tm, D_MODEL), row),
                  _resident((None, 1, D_MODEL), lay3),
                  _resident((None, D_MODEL, D_FF), lay3),
                  _resident((None, D_MODEL, D_FF), lay3),
                  _resident((None, D_FF, D_MODEL), lay3)],
        out_specs=pl.BlockSpec((tm, D_MODEL), row),
        out_shape=jax.ShapeDtypeStruct(h.shape, F32),
        compiler_params=_params(1),
        name="ffn",
    )(h, gain, wg, wu, wd)


def _rope(x, cos_t, sin_lo, sin_hi):
    return (x * cos_t + pltpu.roll(x, LANES - MLA_ROPE // 2, 1) * sin_lo
            + pltpu.roll(x, MLA_ROPE // 2, 1) * sin_hi)


def _proj_body(h_ref, gain_ref, wlat_ref, wb_ref, wc_ref, wgate_ref, qn_ref, kvn_ref,
               wuq_ref, wuk_ref, wuv_ref, sb_ref, sc_ref, cos_ref, slo_ref, shi_ref,
               a_ref, b_ref, c_ref, sg_ref):
    u = _rms(h_ref[...], gain_ref[...]).astype(BF16)

    lat = _dot(u, wlat_ref[...])
    cq = _rms(lat[:, :MLA_Q_LORA], qn_ref[...]).astype(BF16)
    ckv = _rms(lat[:, MLA_Q_LORA:MLA_Q_LORA + MLA_KV_LORA], kvn_ref[...]).astype(BF16)
    kr = lat[:, MLA_Q_LORA + MLA_KV_LORA:]

    cos_t, sin_lo, sin_hi = cos_ref[...], slo_ref[...], shi_ref[...]
    lane = lax.broadcasted_iota(jnp.int32, cos_t.shape, 1)
    scale = (MLA_NOPE + MLA_ROPE) ** -0.5 * LOG2E
    q_cos = (cos_t + jnp.where(lane < MLA_NOPE, 1.0, 0.0)) * scale
    q_lo, q_hi = sin_lo * scale, sin_hi * scale
    kr_rot = _rope(kr, cos_t, sin_lo, sin_hi)

    q = _dot(cq, wuq_ref[...])
    kn = _dot(ckv, wuk_ref[...])
    v = _dot(ckv, wuv_ref[...])
    for hd in range(MLA_HEADS):
        sl = slice(hd * LANES, (hd + 1) * LANES)
        base = (hd // 2) * MLA_STEP_COLS + (hd % 2) * LANES
        a_ref[:, base:base + LANES] = _rope(q[:, sl], q_cos, q_lo, q_hi).astype(BF16)
        a_ref[:, base + 2 * LANES:base + 3 * LANES] = (kn[:, sl] + kr_rot).astype(BF16)
    for pair in range(MLA_HEADS // 2):
        base = pair * MLA_STEP_COLS + 4 * LANES
        a_ref[:, base:base + LANES] = v[:, pair * LANES:(pair + 1) * LANES].astype(BF16)

    b_ref[...] = (_dot(u, wb_ref[...]) * sb_ref[...]).astype(BF16)
    c_ref[...] = (_dot(u, wc_ref[...]) * sc_ref[...]).astype(BF16)
    sg_ref[...] = jax.nn.sigmoid(_dot(u, wgate_ref[...])).astype(BF16)


def _proj(h, layer, seq, gain, wlat, wb, wc, wgate, qn, kvn, wuq, wuk, wuv, sb, sc,
          cos_t, sin_lo, sin_hi):
    n = h.shape[0]
    tm = PROJ_ROWS
    row = lambda r: (r, 0)
    lay3 = lambda r: (layer, 0, 0)
    pos = lambda r: (r % (seq // tm), 0)
    widths = (MLA_HEADS // 2 * MLA_STEP_COLS, SWA_KV_HEADS * SWA_STEP_COLS,
              DIFF_HEADS * DIFF_STEP_COLS, 3 * D_MODEL)
    res = lambda a: _resident((None,) + a.shape[1:], lay3)
    const = lambda a: _resident(a.shape, lambda r: (0, 0))
    return pl.pallas_call(
        _proj_body,
        grid=(n // tm,),
        in_specs=[pl.BlockSpec((tm, D_MODEL), row), res(gain), res(wlat), res(wb), res(wc), res(wgate),
                  res(qn), res(kvn), res(wuq), res(wuk), res(wuv), const(sb), const(sc),
                  pl.BlockSpec((tm, LANES), pos), pl.BlockSpec((tm, LANES), pos),
                  pl.BlockSpec((tm, LANES), pos)],
        out_specs=[pl.BlockSpec((tm, w), row) for w in widths],
        out_shape=[jax.ShapeDtypeStruct((n, w), BF16) for w in widths],
        compiler_params=_params(1),
        name="proj",
    )(h, gain, wlat, wb, wc, wgate, qn, kvn, wuq, wuk, wuv, sb, sc, cos_t, sin_lo, sin_hi)


def _low_half(shape):
    return lax.broadcasted_iota(jnp.int32, shape, len(shape) - 1) < LANES // 2


def _half(x, low, t):
    return jnp.where(low if t == 0 else jnp.logical_not(low), x, jnp.zeros_like(x))


def _with_ones(vx_ref, v):
    vx_ref[:, :LANES] = v
    vx_ref[:, LANES:] = jnp.ones((v.shape[0], LANES), BF16)


def _mla_body(x_ref, o_ref, vx_ref):
    _with_ones(vx_ref, x_ref[:, 4 * LANES:])
    vx = vx_ref[...]
    tq = MLA_Q_ROWS
    for qi in range(x_ref.shape[0] // tq):
        rows = slice(qi * tq, (qi + 1) * tq)
        outs = []
        for j in range(2):
            q = x_ref[rows, j * LANES:(j + 1) * LANES]
            k = x_ref[:, (2 + j) * LANES:(3 + j) * LANES]
            s = _dot_nt(q, k)
            p = jnp.exp2(s - jnp.max(s, axis=-1, keepdims=True)).astype(BF16)
            ol = _dot(p, vx)
            outs.append(ol[:, :LANES] / ol[:, LANES:])
        o_ref[rows, :] = jnp.where(_low_half(outs[0].shape), outs[0], outs[1]).astype(BF16)


def _mla(qkv):
    b, s, _ = qkv.shape
    pair = lambda bi, hp: (bi, 0, hp)
    return pl.pallas_call(
        _mla_body,
        grid=(b, MLA_HEADS // 2),
        in_specs=[pl.BlockSpec((None, s, MLA_STEP_COLS), pair)],
        out_specs=pl.BlockSpec((None, s, LANES), pair),
        out_shape=jax.ShapeDtypeStruct((b, s, MLA_HEADS * MLA_V), BF16),
        scratch_shapes=[pltpu.VMEM((s, 2 * LANES), BF16)],
        compiler_params=_params(2),
        name="mla",
    )(qkv)


def _swa_body(sink_ref, x_ref, bias_ref, o_ref, vx_ref):
    group = pl.program_id(1)
    per_kv = SWA_HEADS // SWA_KV_HEADS
    qb = Q_BLOCK
    n_blocks = x_ref.shape[0] // qb
    _with_ones(vx_ref, x_ref[:, 3 * LANES:])
    low = _low_half((qb, LANES))
    sink = jnp.concatenate([jnp.full((qb, LANES), sink_ref[per_kv * group + t], F32)
                            for t in range(per_kv)], axis=0)
    for j in range(n_blocks):
        rows = slice(j * qb, (j + 1) * qb)
        qa, qc = x_ref[rows, :LANES], x_ref[rows, LANES:2 * LANES]
        lhs = jnp.concatenate([_half(qa, low, 0), _half(qa, low, 1),
                               _half(qc, low, 0), _half(qc, low, 1)], axis=0)
        lo = max(j - 1, 0) * qb
        hi = min(j + 2, n_blocks) * qb
        c0 = lo - (j - 1) * qb
        s = _dot_nt(lhs, x_ref[lo:hi, 2 * LANES:3 * LANES]) + bias_ref[:, c0:c0 + (hi - lo)]
        m = jnp.maximum(jnp.max(s, axis=-1, keepdims=True), sink)
        p = jnp.exp2(s - jnp.concatenate([m] * ((hi - lo) // LANES), axis=1)).astype(BF16)
        ol = _dot(p, vx_ref[lo:hi, :])
        o = ol[:, :LANES] / (ol[:, LANES:] + jnp.exp2(sink - m))
        o_ref[rows, :LANES] = jnp.where(low, o[:qb], o[qb:2 * qb]).astype(BF16)
        o_ref[rows, LANES:] = jnp.where(low, o[2 * qb:3 * qb], o[3 * qb:]).astype(BF16)


def _swa(sink, qkv, bias):
    b, s, _ = qkv.shape
    per_kv = SWA_HEADS // SWA_KV_HEADS
    return pl.pallas_call(
        _swa_body,
        grid=(b, SWA_KV_HEADS),
        in_specs=[pl.BlockSpec(memory_space=pltpu.SMEM),
                  pl.BlockSpec((None, s, SWA_STEP_COLS), lambda bi, g: (bi, 0, g)),
                  pl.BlockSpec((None, per_kv * Q_BLOCK, 3 * Q_BLOCK), lambda bi, g: (g, 0, 0))],
        out_specs=pl.BlockSpec((None, s, 2 * LANES), lambda bi, g: (bi, 0, g)),
        out_shape=jax.ShapeDtypeStruct((b, s, SWA_HEADS * SWA_HEAD_DIM), BF16),
        scratch_shapes=[pltpu.VMEM((s, 2 * LANES), BF16)],
        compiler_params=_params(2),
        name="swa",
    )(sink, qkv, bias)


def _diff_body(lam_ref, far_ref, x_ref, near_ref, subln_ref, o_ref, vx_ref, *, lambda_init):
    head = pl.program_id(0)
    seq = x_ref.shape[0]
    tq = DIFF_Q_ROWS
    lv = lam_ref[...]
    lam = (jnp.exp(jnp.sum(lv[0:1] * lv[1:2], axis=-1, keepdims=True))
           - jnp.exp(jnp.sum(lv[2:3] * lv[3:4], axis=-1, keepdims=True)) + lambda_init)
    c_left, c_right = far_ref[2 * head], far_ref[2 * head + 1]
    low = _low_half((tq, LANES))
    k = x_ref[:, LANES:2 * LANES]
    _with_ones(vx_ref, x_ref[:, 2 * LANES:])
    vx = vx_ref[...]
    for qt in range(seq // tq):
        rows = slice(qt * tq, (qt + 1) * tq)
        q = x_ref[rows, :LANES]
        a = max(0, qt - 1) * tq
        b = min(seq // tq, qt + 2) * tq
        t0 = a - (qt - 1) * tq
        maps = []
        for t in range(2):
            s = _dot_nt(_half(q, low, t), k)
            near = s[:, a:b] + near_ref[:, t0:t0 + (b - a)]
            m = jnp.max(near, axis=-1, keepdims=True)
            if a > 0:
                m = jnp.maximum(m, jnp.max(s[:, :a], axis=-1, keepdims=True) + c_left)
            if b < seq:
                m = jnp.maximum(m, jnp.max(s[:, b:], axis=-1, keepdims=True) + c_right)
            parts = []
            if a > 0:
                parts.append(jnp.exp2(s[:, :a] - (m - c_left)).astype(BF16))
            parts.append(jnp.exp2(near - m).astype(BF16))
            if b < seq:
                parts.append(jnp.exp2(s[:, b:] - (m - c_right)).astype(BF16))
            ol = _dot(jnp.concatenate(parts, axis=1), vx)
            maps.append(ol[:, :LANES] / ol[:, LANES:])
        o = maps[0] - lam * maps[1]
        o_ref[rows, :] = (_rms(o, subln_ref[...]) * (1.0 - lambda_init)).astype(BF16)


def _diff(layer, lambda_init, lam_vecs, far, qkv, near, subln):
    b, s, _ = qkv.shape
    head_block = lambda hd, bi: (bi, 0, hd)
    return pl.pallas_call(
        functools.partial(_diff_body, lambda_init=lambda_init),
        grid=(DIFF_HEADS, b),
        in_specs=[pl.BlockSpec((None, 4, DIFF_HEAD_DIM), lambda hd, bi: (layer, 0, 0)),
                  pl.BlockSpec(memory_space=pltpu.SMEM),
                  pl.BlockSpec((None, s, DIFF_STEP_COLS), head_block),
                  pl.BlockSpec((None, DIFF_Q_ROWS, 3 * DIFF_Q_ROWS), lambda hd, bi: (hd, 0, 0)),
                  pl.BlockSpec((None, 1, LANES), lambda hd, bi: (layer, 0, 0))],
        out_specs=pl.BlockSpec((None, s, LANES), head_block),
        out_shape=jax.ShapeDtypeStruct((b, s, DIFF_HEADS * 2 * DIFF_HEAD_DIM), BF16),
        scratch_shapes=[pltpu.VMEM((s, 2 * LANES), BF16)],
        compiler_params=_params(2),
        name="diff",
    )(lam_vecs, far, qkv, near, subln)


def _post_body(h_ref, sg_ref, oa_ref, ob_ref, oc_ref, wo_ref, gain_ref, wg_ref, wu_ref, wd_ref,
               p_ref, pgain_ref, wpg_ref, wpp_ref, fin_ref, o_ref, *, final):
    d = D_MODEL
    merged = (sg_ref[:, :d].astype(F32) * oa_ref[...].astype(F32)
              + sg_ref[:, d:2 * d].astype(F32) * ob_ref[...].astype(F32)
              + sg_ref[:, 2 * d:].astype(F32) * oc_ref[...].astype(F32))
    h = h_ref[...] + _dot(merged.astype(BF16), wo_ref[...])
    h = _swiglu_half_step(h, gain_ref, wg_ref, wu_ref, wd_ref)
    gate = jax.nn.sigmoid(_dot(_rms(h, pgain_ref[...]).astype(BF16), wpg_ref[...]))
    h = h + gate * _dot(p_ref[...].astype(BF16), wpp_ref[...])
    o_ref[...] = _rms(h, fin_ref[...]) if final else h


def _post(h, layer, sg, oa, ob, oc, w_out, gain, wg, wu, wd, p, pgain, wpg, wpp, final_norm, final):
    n = h.shape[0]
    tm = POST_ROWS
    row = lambda r: (r, 0)
    lay3 = lambda r: (layer, 0, 0)
    res = lambda a: _resident((None,) + a.shape[1:], lay3)
    return pl.pallas_call(
        functools.partial(_post_body, final=final),
        grid=(n // tm,),
        in_specs=[pl.BlockSpec((tm, D_MODEL), row), pl.BlockSpec((tm, 3 * D_MODEL), row),
                  pl.BlockSpec((tm, D_MODEL), row), pl.BlockSpec((tm, D_MODEL), row),
                  pl.BlockSpec((tm, D_MODEL), row),
                  res(w_out), res(gain), res(wg), res(wu), res(wd),
                  pl.BlockSpec((None, tm, PLE_DIM), lambda r: (layer, r, 0)),
                  res(pgain), res(wpg), res(wpp), _resident((1, D_MODEL), lambda r: (0, 0))],
        out_specs=pl.BlockSpec((tm, D_MODEL), row),
        out_shape=jax.ShapeDtypeStruct(h.shape, F32),
        compiler_params=_params(1),
        name="post",
    )(h, sg, oa, ob, oc, w_out, gain, wg, wu, wd, p, pgain, wpg, wpp, final_norm)


def _bias_body(table_ref, idx_ref, o_ref, *, first_head, n_heads):
    head = first_head + pl.program_id(0)
    idx = idx_ref[...]
    acc = jnp.full(idx.shape, NEG_INF, F32)
    for bucket in range(REL_BUCKETS):
        acc = jnp.where(idx == bucket, table_ref[bucket * n_heads + head] * LOG2E, acc)
    o_ref[...] = acc


def _bias_tiles(table_flat, idx, first_head, count, n_heads):
    return pl.pallas_call(
        functools.partial(_bias_body, first_head=first_head, n_heads=n_heads),
        grid=(count,),
        in_specs=[pl.BlockSpec(memory_space=pltpu.SMEM),
                  _resident(idx.shape, lambda hd: (0, 0))],
        out_specs=pl.BlockSpec((None,) + idx.shape, lambda hd: (hd, 0, 0)),
        out_shape=jax.ShapeDtypeStruct((count,) + idx.shape, F32),
        compiler_params=_params(1),
        name="bias",
    )(table_flat, idx)


def _t5_bucket_np(rel):
    nb = REL_BUCKETS // 2
    max_exact = nb // 2
    base = np.where(rel > 0, nb, 0)
    n = np.abs(rel)
    nf = np.maximum(n, 1).astype(np.float64)
    large = max_exact + (np.log(nf / max_exact) / math.log(REL_MAX_DIST / max_exact)
                         * (nb - max_exact)).astype(np.int64)
    large = np.minimum(large, nb - 1)
    return (base + np.where(n < max_exact, n, large)).astype(np.int32)


def _bias_indices():
    rel = (np.arange(3 * Q_BLOCK)[None, :] - WINDOW) - np.arange(Q_BLOCK)[:, None]
    idx_b = np.where(np.abs(rel) <= WINDOW, _t5_bucket_np(rel), REL_BUCKETS).astype(np.int32)
    t = DIFF_Q_ROWS
    rel = (np.arange(3 * t)[None, :] - t) - np.arange(t)[:, None]
    idx_c = _t5_bucket_np(rel)
    far = _t5_bucket_np(np.array([-t, t]))
    assert (far == _t5_bucket_np(np.array([-REL_MAX_DIST, REL_MAX_DIST]))).all()
    return idx_b, idx_c, far


def _rope_tables(seq):
    half = MLA_ROPE // 2
    inv = ROPE_THETA ** (-jnp.arange(0, MLA_ROPE, 2, dtype=F32) / MLA_ROPE)
    ang = jnp.arange(seq, dtype=F32)[:, None] * inv[None, :]
    cos, sin = jnp.cos(ang), jnp.sin(ang)
    z = lambda w: jnp.zeros((seq, w), F32)
    cos_t = jnp.concatenate([z(MLA_NOPE), cos, cos, z(LANES - MLA_NOPE - MLA_ROPE)], axis=1)
    sin_lo = jnp.concatenate([z(MLA_NOPE), -sin, z(LANES - MLA_NOPE - half)], axis=1)
    sin_hi = jnp.concatenate([z(MLA_NOPE + half), sin, z(LANES - MLA_NOPE - MLA_ROPE)], axis=1)
    return cos_t, sin_lo, sin_hi


def _dup_heads(w, heads, dim):
    w4 = w.reshape(w.shape[:2] + (heads, dim))
    return jnp.concatenate([w4, w4], axis=-1).reshape(w.shape[:2] + (heads * 2 * dim,))


def kernel(x, p, ffn1_norm, ffn1_w_gate, ffn1_w_up, ffn1_w_down, mix_norm, w_in, mla_q_norm, mla_w_uq, mla_kv_norm, mla_w_ukv, swa_sink, diff_lambda_q1, diff_lambda_k1, diff_lambda_q2, diff_lambda_k2, diff_subln, rel_table, w_out, ffn2_norm, ffn2_w_gate, ffn2_w_up, ffn2_w_down, ple_norm, ple_w_gate, ple_w_proj, final_norm):
    b, s, d = x.shape
    depth = w_in.shape[0]
    n = b * s
    bf = lambda a: a.astype(BF16)
    vec = lambda a: a.astype(F32)[:, None, :]

    o = np.cumsum((0, MLA_Q_LORA, MLA_KV_LORA, MLA_ROPE, 1024, 256, 256, 1024, 1024, 1024, 3072))
    col = lambda i: w_in[:, :, o[i]:o[i + 1]]
    zeros = lambda w: jnp.zeros((depth, d, w), w_in.dtype)
    wlat = bf(jnp.concatenate([col(0), col(1), zeros(MLA_NOPE), col(2),
                               zeros(LANES - MLA_NOPE - MLA_ROPE)], axis=-1))
    by = lambda w, groups: w.reshape(depth, d, groups, w.shape[-1] // groups)
    wb = bf(jnp.concatenate([by(col(3), SWA_KV_HEADS),
                             by(_dup_heads(col(4), SWA_KV_HEADS, SWA_HEAD_DIM), SWA_KV_HEADS),
                             by(_dup_heads(col(5), SWA_KV_HEADS, SWA_HEAD_DIM), SWA_KV_HEADS)],
                            axis=-1).reshape(depth, d, SWA_KV_HEADS * SWA_STEP_COLS))
    wc = bf(jnp.concatenate([by(col(6), DIFF_HEADS), by(col(7), DIFF_HEADS), by(col(8), DIFF_HEADS)],
                            axis=-1).reshape(depth, d, DIFF_HEADS * DIFF_STEP_COLS))
    q_scale = lambda dim, q_cols, step_cols, steps: jnp.tile(jnp.concatenate(
        [jnp.full((q_cols,), dim ** -0.5 * LOG2E, F32), jnp.ones((step_cols - q_cols,), F32)]),
        steps)[None, :]
    sb = q_scale(SWA_HEAD_DIM, 2 * LANES, SWA_STEP_COLS, SWA_KV_HEADS)
    sc = q_scale(DIFF_HEAD_DIM, LANES, DIFF_STEP_COLS, DIFF_HEADS)
    wgate = bf(col(9))
    uq = mla_w_uq.reshape(depth, MLA_Q_LORA, MLA_HEADS, MLA_NOPE + MLA_ROPE)
    wuq = bf(jnp.pad(uq, ((0, 0),) * 3 + ((0, LANES - MLA_NOPE - MLA_ROPE),))
             .reshape(depth, MLA_Q_LORA, MLA_HEADS * LANES))
    ukv = mla_w_ukv.reshape(depth, MLA_KV_LORA, MLA_HEADS, MLA_NOPE + MLA_V)
    wuk = bf(jnp.pad(ukv[..., :MLA_NOPE], ((0, 0),) * 3 + ((0, LANES - MLA_NOPE),))
             .reshape(depth, MLA_KV_LORA, MLA_HEADS * LANES))
    wuv = bf(ukv[..., MLA_NOPE:].reshape(depth, MLA_KV_LORA, MLA_HEADS * MLA_V))

    cos_t, sin_lo, sin_hi = _rope_tables(s)
    idx_b, idx_c, far_buckets = _bias_indices()
    n_bias_heads = SWA_HEADS + DIFF_HEADS
    table_flat = rel_table.astype(F32).reshape(REL_BUCKETS * n_bias_heads)
    per_kv = SWA_HEADS // SWA_KV_HEADS
    bias_b = _bias_tiles(table_flat, jnp.asarray(idx_b), 0, SWA_HEADS, n_bias_heads).reshape(
        SWA_KV_HEADS, per_kv * Q_BLOCK, 3 * Q_BLOCK)
    near_c = _bias_tiles(table_flat, jnp.asarray(idx_c), SWA_HEADS, DIFF_HEADS, n_bias_heads)
    far_c = (jnp.stack([rel_table[int(far_buckets[0]), SWA_HEADS:],
                        rel_table[int(far_buckets[1]), SWA_HEADS:]], axis=1).astype(F32)
             * LOG2E).reshape(2 * DIFF_HEADS)
    sink = swa_sink.astype(F32) * LOG2E
    lam_vecs = jnp.stack([diff_lambda_q1, diff_lambda_k1, diff_lambda_q2, diff_lambda_k2],
                         axis=1).astype(F32)
    f1 = (vec(ffn1_norm), bf(ffn1_w_gate), bf(ffn1_w_up), bf(ffn1_w_down))
    f2 = (vec(ffn2_norm), bf(ffn2_w_gate), bf(ffn2_w_up), bf(ffn2_w_down))
    mixn, qn, kvn, subln, plen = (vec(mix_norm), vec(mla_q_norm), vec(mla_kv_norm),
                                  vec(diff_subln), vec(ple_norm))
    wo, wpg, wpp = bf(w_out), bf(ple_w_gate), bf(ple_w_proj)
    fin = final_norm.astype(F32)[None, :]
    p2 = p.reshape(depth, n, PLE_DIM)

    h = x.reshape(n, d).astype(F32)
    for i in range(depth):
        h = _ffn(h, i, *f1)
        qkv_a, qkv_b, qkv_c, sg = _proj(h, i, s, mixn, wlat, wb, wc, wgate, qn, kvn, wuq, wuk, wuv,
                                        sb, sc, cos_t, sin_lo, sin_hi)
        r3 = lambda a: a.reshape(b, s, a.shape[-1])
        oa = _mla(r3(qkv_a))
        ob = _swa(sink[i], r3(qkv_b), bias_b)
        lambda_init = 0.8 - 0.6 * math.exp(-0.3 * i)
        oc = _diff(i, lambda_init, lam_vecs, far_c, r3(qkv_c), near_c, subln)
        h = _post(h, i, sg, oa.reshape(n, d), ob.reshape(n, d), oc.reshape(n, d), wo, *f2,
                  p2, plen, wpg, wpp, fin, final=(i == depth - 1))
    return h.reshape(b, s, d).astype(x.dtype)
```

```python
import functools
import math

import jax
import jax.numpy as jnp
import numpy as np
from jax import lax
from jax.experimental import pallas as pl
from jax.experimental.pallas import tpu as pltpu

F32 = jnp.float32
BF16 = jnp.bfloat16

D_MODEL = 1024
D_FF = 2816
PLE_DIM = 256
NORM_EPS = 1e-6
NEG_INF = -1e30
ROPE_THETA = 10000.0
MLA_HEADS = 16
MLA_NOPE = 64
MLA_ROPE = 32
MLA_V = 64
MLA_Q_LORA = 384
MLA_KV_LORA = 256
SWA_HEADS = 16
SWA_KV_HEADS = 4
SWA_HEAD_DIM = 64
WINDOW = 128
Q_BLOCK = 128
DIFF_HEADS = 8
DIFF_HEAD_DIM = 64
REL_BUCKETS = 32
REL_MAX_DIST = 128
LOG2E = math.log2(math.e)

LANES = 128
VMEM_LIMIT = 56 * 1024 * 1024

FFN_ROWS = 512
PROJ_ROWS = 256
POST_ROWS = 512
MLA_Q_ROWS = 128
DIFF_Q_ROWS = 128
MLA_SCORE_LEAD = 4
DIFF_SCORE_LEAD = 2
MLA_STEP_COLS = 5 * LANES
SWA_STEP_COLS = 4 * LANES
DIFF_STEP_COLS = 3 * LANES


def _params(n_axes):
    return pltpu.CompilerParams(dimension_semantics=("arbitrary",) * n_axes,
                                vmem_limit_bytes=VMEM_LIMIT)


def _resident(shape, index_map):
    return pl.BlockSpec(shape, index_map, pipeline_mode=pl.Buffered(1))


def _rms(x, gain):
    return x * lax.rsqrt(jnp.mean(x * x, axis=-1, keepdims=True) + NORM_EPS) * gain


def _dot(a, b):
    return jnp.dot(a, b, preferred_element_type=F32)


def _dot_nt(a, b):
    return lax.dot_general(a, b, (((1,), (1,)), ((), ())), preferred_element_type=F32)


def _swiglu_half_step(h, gain_ref, wg_ref, wu_ref, wd_ref):
    n = _rms(h, gain_ref[...]).astype(BF16)
    g = _dot(n, wg_ref[...])
    u = _dot(n, wu_ref[...])
    a = (g * jax.nn.sigmoid(g) * u).astype(BF16)
    return h + 0.5 * _dot(a, wd_ref[...])


def _ffn_body(h_ref, gain_ref, wg_ref, wu_ref, wd_ref, o_ref):
    o_ref[...] = _swiglu_half_step(h_ref[...], gain_ref, wg_ref, wu_ref, wd_ref)


def _ffn(h, layer, gain, wg, wu, wd):
    n = h.shape[0]
    tm = FFN_ROWS
    row = lambda r: (r, 0)
    lay3 = lambda r: (layer, 0, 0)
    return pl.pallas_call(
        _ffn_body,
        grid=(n // tm,),
        in_specs=[pl.BlockSpec((tm, D_MODEL), row),
                  _resident((None, 1, D_MODEL), lay3),
                  _resident((None, D_MODEL, D_FF), lay3),
                  _resident((None, D_MODEL, D_FF), lay3),
                  _resident((None, D_FF, D_MODEL), lay3)],
        out_specs=pl.BlockSpec((tm, D_MODEL), row),
        out_shape=jax.ShapeDtypeStruct(h.shape, F32),
        compiler_params=_params(1),
        name="ffn",
    )(h, gain, wg, wu, wd)


def _rope(x, cos_t, sin_lo, sin_hi):
    return (x * cos_t + pltpu.roll(x, LANES - MLA_ROPE // 2, 1) * sin_lo
            + pltpu.roll(x, MLA_ROPE // 2, 1) * sin_hi)


def _proj_body(h_ref, gain_ref, wlat_ref, wb_ref, wc_ref, wgate_ref, qn_ref, kvn_ref,
               wuq_ref, wuk_ref, wuv_ref, sb_ref, sc_ref, cos_ref, slo_ref, shi_ref,
               a_ref, b_ref, c_ref, sg_ref):
    u = _rms(h_ref[...], gain_ref[...]).astype(BF16)

    lat = _dot(u, wlat_ref[...])
    b_ref[...] = (_dot(u, wb_ref[...]) * sb_ref[...]).astype(BF16)
    cq = _rms(lat[:, :MLA_Q_LORA], qn_ref[...]).astype(BF16)
    ckv = _rms(lat[:, MLA_Q_LORA:MLA_Q_LORA + MLA_KV_LORA], kvn_ref[...]).astype(BF16)
    kr = lat[:, MLA_Q_LORA + MLA_KV_LORA:]

    cos_t, sin_lo, sin_hi = cos_ref[...], slo_ref[...], shi_ref[...]
    lane = lax.broadcasted_iota(jnp.int32, cos_t.shape, 1)
    scale = (MLA_NOPE + MLA_ROPE) ** -0.5 * LOG2E
    q_cos = (cos_t + jnp.where(lane < MLA_NOPE, 1.0, 0.0)) * scale
    q_lo, q_hi = sin_lo * scale, sin_hi * scale
    kr_rot = _rope(kr, cos_t, sin_lo, sin_hi)

    q = _dot(cq, wuq_ref[...])
    kn = _dot(ckv, wuk_ref[...])
    v = _dot(ckv, wuv_ref[...])
    for hd in range(MLA_HEADS):
        sl = slice(hd * LANES, (hd + 1) * LANES)
        base = (hd // 2) * MLA_STEP_COLS + (hd % 2) * LANES
        a_ref[:, base:base + LANES] = _rope(q[:, sl], q_cos, q_lo, q_hi).astype(BF16)
        a_ref[:, base + 2 * LANES:base + 3 * LANES] = (kn[:, sl] + kr_rot).astype(BF16)
    for pair in range(MLA_HEADS // 2):
        base = pair * MLA_STEP_COLS + 4 * LANES
        a_ref[:, base:base + LANES] = v[:, pair * LANES:(pair + 1) * LANES].astype(BF16)

    c_ref[...] = (_dot(u, wc_ref[...]) * sc_ref[...]).astype(BF16)
    sg_ref[...] = jax.nn.sigmoid(_dot(u, wgate_ref[...])).astype(BF16)


def _proj(h, layer, seq, gain, wlat, wb, wc, wgate, qn, kvn, wuq, wuk, wuv, sb, sc,
          cos_t, sin_lo, sin_hi):
    n = h.shape[0]
    tm = PROJ_ROWS
    row = lambda r: (r, 0)
    lay3 = lambda r: (layer, 0, 0)
    pos = lambda r: (r % (seq // tm), 0)
    widths = (MLA_HEADS // 2 * MLA_STEP_COLS, SWA_KV_HEADS * SWA_STEP_COLS,
              DIFF_HEADS * DIFF_STEP_COLS, 3 * D_MODEL)
    res = lambda a: _resident((None,) + a.shape[1:], lay3)
    const = lambda a: _resident(a.shape, lambda r: (0, 0))
    return pl.pallas_call(
        _proj_body,
        grid=(n // tm,),
        in_specs=[pl.BlockSpec((tm, D_MODEL), row), res(gain), res(wlat), res(wb), res(wc), res(wgate),
                  res(qn), res(kvn), res(wuq), res(wuk), res(wuv), const(sb), const(sc),
                  pl.BlockSpec((tm, LANES), pos), pl.BlockSpec((tm, LANES), pos),
                  pl.BlockSpec((tm, LANES), pos)],
        out_specs=[pl.BlockSpec((tm, w), row) for w in widths],
        out_shape=[jax.ShapeDtypeStruct((n, w), BF16) for w in widths],
        compiler_params=_params(1),
        name="proj",
    )(h, gain, wlat, wb, wc, wgate, qn, kvn, wuq, wuk, wuv, sb, sc, cos_t, sin_lo, sin_hi)


def _low_half(shape):
    return lax.broadcasted_iota(jnp.int32, shape, len(shape) - 1) < LANES // 2


def _half(x, low, t):
    return jnp.where(low if t == 0 else jnp.logical_not(low), x, jnp.zeros_like(x))


def _with_ones(vx_ref, v):
    vx_ref[:, :LANES] = v
    vx_ref[:, LANES:] = jnp.ones((v.shape[0], LANES), BF16)


def _skewed(n_units, lead, score_unit, value_unit):
    for u in range(n_units + lead):
        if u < n_units:
            score_unit(u)
        if u >= lead:
            value_unit(u - lead)


def _mla_body(x_ref, o_ref, vx_ref):
    _with_ones(vx_ref, x_ref[:, 4 * LANES:])
    vx = vx_ref[...]
    tq = MLA_Q_ROWS
    low = _low_half((tq, LANES))
    scores, outs = {}, {}

    def score_unit(u):
        qi, j = divmod(u, 2)
        q = x_ref[qi * tq:(qi + 1) * tq, j * LANES:(j + 1) * LANES]
        scores[u] = _dot_nt(q, x_ref[:, (2 + j) * LANES:(3 + j) * LANES])

    def value_unit(u):
        qi, j = divmod(u, 2)
        s = scores.pop(u)
        p = jnp.exp2(s - jnp.max(s, axis=-1, keepdims=True)).astype(BF16)
        ol = _dot(p, vx)
        outs[j] = ol[:, :LANES] / ol[:, LANES:]
        if j == 1:
            o_ref[qi * tq:(qi + 1) * tq, :] = jnp.where(low, outs[0], outs[1]).astype(BF16)

    _skewed(2 * (x_ref.shape[0] // tq), MLA_SCORE_LEAD, score_unit, value_unit)


def _mla(qkv):
    b, s, _ = qkv.shape
    pair = lambda bi, hp: (bi, 0, hp)
    return pl.pallas_call(
        _mla_body,
        grid=(b, MLA_HEADS // 2),
        in_specs=[pl.BlockSpec((None, s, MLA_STEP_COLS), pair)],
        out_specs=pl.BlockSpec((None, s, LANES), pair),
        out_shape=jax.ShapeDtypeStruct((b, s, MLA_HEADS * MLA_V), BF16),
        scratch_shapes=[pltpu.VMEM((s, 2 * LANES), BF16)],
        compiler_params=_params(2),
        name="mla",
    )(qkv)


def _swa_body(sink_ref, x_ref, bias_ref, o_ref, vx_ref):
    group = pl.program_id(1)
    per_kv = SWA_HEADS // SWA_KV_HEADS
    qb = Q_BLOCK
    n_blocks = x_ref.shape[0] // qb
    _with_ones(vx_ref, x_ref[:, 3 * LANES:])
    low = _low_half((qb, LANES))
    sink = jnp.concatenate([jnp.full((qb, LANES), sink_ref[per_kv * group + t], F32)
                            for t in range(per_kv)], axis=0)
    for j in range(n_blocks):
        rows = slice(j * qb, (j + 1) * qb)
        qa, qc = x_ref[rows, :LANES], x_ref[rows, LANES:2 * LANES]
        lhs = jnp.concatenate([_half(qa, low, 0), _half(qa, low, 1),
                               _half(qc, low, 0), _half(qc, low, 1)], axis=0)
        lo = max(j - 1, 0) * qb
        hi = min(j + 2, n_blocks) * qb
        c0 = lo - (j - 1) * qb
        s = _dot_nt(lhs, x_ref[lo:hi, 2 * LANES:3 * LANES]) + bias_ref[:, c0:c0 + (hi - lo)]
        m = jnp.maximum(jnp.max(s, axis=-1, keepdims=True), sink)
        p = jnp.exp2(s - jnp.concatenate([m] * ((hi - lo) // LANES), axis=1)).astype(BF16)
        ol = _dot(p, vx_ref[lo:hi, :])
        o = ol[:, :LANES] / (ol[:, LANES:] + jnp.exp2(sink - m))
        o_ref[rows, :LANES] = jnp.where(low, o[:qb], o[qb:2 * qb]).astype(BF16)
        o_ref[rows, LANES:] = jnp.where(low, o[2 * qb:3 * qb], o[3 * qb:]).astype(BF16)


def _swa(sink, qkv, bias):
    b, s, _ = qkv.shape
    per_kv = SWA_HEADS // SWA_KV_HEADS
    return pl.pallas_call(
        _swa_body,
        grid=(b, SWA_KV_HEADS),
        in_specs=[pl.BlockSpec(memory_space=pltpu.SMEM),
                  pl.BlockSpec((None, s, SWA_STEP_COLS), lambda bi, g: (bi, 0, g)),
                  pl.BlockSpec((None, per_kv * Q_BLOCK, 3 * Q_BLOCK), lambda bi, g: (g, 0, 0))],
        out_specs=pl.BlockSpec((None, s, 2 * LANES), lambda bi, g: (bi, 0, g)),
        out_shape=jax.ShapeDtypeStruct((b, s, SWA_HEADS * SWA_HEAD_DIM), BF16),
        scratch_shapes=[pltpu.VMEM((s, 2 * LANES), BF16)],
        compiler_params=_params(2),
        name="swa",
    )(sink, qkv, bias)


def _diff_body(lam_ref, far_ref, x_ref, near_ref, subln_ref, o_ref, vx_ref, *, lambda_init):
    head = pl.program_id(0)
    seq = x_ref.shape[0]
    tq = DIFF_Q_ROWS
    lv = lam_ref[...]
    lam = (jnp.exp(jnp.sum(lv[0:1] * lv[1:2], axis=-1, keepdims=True))
           - jnp.exp(jnp.sum(lv[2:3] * lv[3:4], axis=-1, keepdims=True)) + lambda_init)
    c_left, c_right = far_ref[2 * head], far_ref[2 * head + 1]
    low = _low_half((tq, LANES))
    k = x_ref[:, LANES:2 * LANES]
    _with_ones(vx_ref, x_ref[:, 2 * LANES:])
    vx = vx_ref[...]
    scores, maps = {}, {}

    def score_unit(u):
        qt, t = divmod(u, 2)
        q = x_ref[qt * tq:(qt + 1) * tq, :LANES]
        scores[u] = _dot_nt(_half(q, low, t), k)

    def value_unit(u):
        qt, t = divmod(u, 2)
        s = scores.pop(u)
        a = max(0, qt - 1) * tq
        b = min(seq // tq, qt + 2) * tq
        t0 = a - (qt - 1) * tq
        near = s[:, a:b] + near_ref[:, t0:t0 + (b - a)]
        m = jnp.max(near, axis=-1, keepdims=True)
        if a > 0:
            m = jnp.maximum(m, jnp.max(s[:, :a], axis=-1, keepdims=True) + c_left)
        if b < seq:
            m = jnp.maximum(m, jnp.max(s[:, b:], axis=-1, keepdims=True) + c_right)
        parts = []
        if a > 0:
            parts.append(jnp.exp2(s[:, :a] - (m - c_left)).astype(BF16))
        parts.append(jnp.exp2(near - m).astype(BF16))
        if b < seq:
            parts.append(jnp.exp2(s[:, b:] - (m - c_right)).astype(BF16))
        ol = _dot(jnp.concatenate(parts, axis=1), vx)
        maps[t] = ol[:, :LANES] / ol[:, LANES:]
        if t == 1:
            o = maps[0] - lam * maps[1]
            o_ref[qt * tq:(qt + 1) * tq, :] = (_rms(o, subln_ref[...])
                                               * (1.0 - lambda_init)).astype(BF16)

    _skewed(2 * (seq // tq), DIFF_SCORE_LEAD, score_unit, value_unit)


def _diff(layer, lambda_init, lam_vecs, far, qkv, near, subln):
    b, s, _ = qkv.shape
    head_block = lambda hd, bi: (bi, 0, hd)
    return pl.pallas_call(
        functools.partial(_diff_body, lambda_init=lambda_init),
        grid=(DIFF_HEADS, b),
        in_specs=[pl.BlockSpec((None, 4, DIFF_HEAD_DIM), lambda hd, bi: (layer, 0, 0)),
                  pl.BlockSpec(memory_space=pltpu.SMEM),
                  pl.BlockSpec((None, s, DIFF_STEP_COLS), head_block),
                  pl.BlockSpec((None, DIFF_Q_ROWS, 3 * DIFF_Q_ROWS), lambda hd, bi: (hd, 0, 0)),
                  pl.BlockSpec((None, 1, LANES), lambda hd, bi: (layer, 0, 0))],
        out_specs=pl.BlockSpec((None, s, LANES), head_block),
        out_shape=jax.ShapeDtypeStruct((b, s, DIFF_HEADS * 2 * DIFF_HEAD_DIM), BF16),
        scratch_shapes=[pltpu.VMEM((s, 2 * LANES), BF16)],
        compiler_params=_params(2),
        name="diff",
    )(lam_vecs, far, qkv, near, subln)


def _post_body(h_ref, sg_ref, oa_ref, ob_ref, oc_ref, wo_ref, gain_ref, wg_ref, wu_ref, wd_ref,
               p_ref, pgain_ref, wpg_ref, wpp_ref, fin_ref, o_ref, *, final):
    d = D_MODEL
    merged = (sg_ref[:, :d].astype(F32) * oa_ref[...].astype(F32)
              + sg_ref[:, d:2 * d].astype(F32) * ob_ref[...].astype(F32)
              + sg_ref[:, 2 * d:].astype(F32) * oc_ref[...].astype(F32))
    h = h_ref[...] + _dot(merged.astype(BF16), wo_ref[...])
    h = _swiglu_half_step(h, gain_ref, wg_ref, wu_ref, wd_ref)
    gate = jax.nn.sigmoid(_dot(_rms(h, pgain_ref[...]).astype(BF16), wpg_ref[...]))
    h = h + gate * _dot(p_ref[...].astype(BF16), wpp_ref[...])
    o_ref[...] = _rms(h, fin_ref[...]) if final else h


def _post(h, layer, sg, oa, ob, oc, w_out, gain, wg, wu, wd, p, pgain, wpg, wpp, final_norm, final):
    n = h.shape[0]
    tm = POST_ROWS
    row = lambda r: (r, 0)
    lay3 = lambda r: (layer, 0, 0)
    res = lambda a: _resident((None,) + a.shape[1:], lay3)
    return pl.pallas_call(
        functools.partial(_post_body, final=final),
        grid=(n // tm,),
        in_specs=[pl.BlockSpec((tm, D_MODEL), row), pl.BlockSpec((tm, 3 * D_MODEL), row),
                  pl.BlockSpec((tm, D_MODEL), row), pl.BlockSpec((tm, D_MODEL), row),
                  pl.BlockSpec((tm, D_MODEL), row),
                  res(w_out), res(gain), res(wg), res(wu), res(wd),
                  pl.BlockSpec((None, tm, PLE_DIM), lambda r: (layer, r, 0)),
                  res(pgain), res(wpg), res(wpp), _resident((1, D_MODEL), lambda r: (0, 0))],
        out_specs=pl.BlockSpec((tm, D_MODEL), row),
        out_shape=jax.ShapeDtypeStruct(h.shape, F32),
        compiler_params=_params(1),
        name="post",
    )(h, sg, oa, ob, oc, w_out, gain, wg, wu, wd, p, pgain, wpg, wpp, final_norm)


def _bias_body(table_ref, idx_ref, o_ref, *, first_head, n_heads):
    head = first_head + pl.program_id(0)
    idx = idx_ref[...]
    acc = jnp.full(idx.shape, NEG_INF, F32)
    for bucket in range(REL_BUCKETS):
        acc = jnp.where(idx == bucket, table_ref[bucket * n_heads + head] * LOG2E, acc)
    o_ref[...] = acc


def _bias_tiles(table_flat, idx, first_head, count, n_heads):
    return pl.pallas_call(
        functools.partial(_bias_body, first_head=first_head, n_heads=n_heads),
        grid=(count,),
        in_specs=[pl.BlockSpec(memory_space=pltpu.SMEM),
                  _resident(idx.shape, lambda hd: (0, 0))],
        out_specs=pl.BlockSpec((None,) + idx.shape, lambda hd: (hd, 0, 0)),
        out_shape=jax.ShapeDtypeStruct((count,) + idx.shape, F32),
        compiler_params=_params(1),
        name="bias",
    )(table_flat, idx)


def _t5_bucket_np(rel):
    nb = REL_BUCKETS // 2
    max_exact = nb // 2
    base = np.where(rel > 0, nb, 0)
    n = np.abs(rel)
    nf = np.maximum(n, 1).astype(np.float64)
    large = max_exact + (np.log(nf / max_exact) / math.log(REL_MAX_DIST / max_exact)
                         * (nb - max_exact)).astype(np.int64)
    large = np.minimum(large, nb - 1)
    return (base + np.where(n < max_exact, n, large)).astype(np.int32)


def _bias_indices():
    rel = (np.arange(3 * Q_BLOCK)[None, :] - WINDOW) - np.arange(Q_BLOCK)[:, None]
    idx_b = np.where(np.abs(rel) <= WINDOW, _t5_bucket_np(rel), REL_BUCKETS).astype(np.int32)
    t = DIFF_Q_ROWS
    rel = (np.arange(3 * t)[None, :] - t) - np.arange(t)[:, None]
    idx_c = _t5_bucket_np(rel)
    far = _t5_bucket_np(np.array([-t, t]))
    assert (far == _t5_bucket_np(np.array([-REL_MAX_DIST, REL_MAX_DIST]))).all()
    return idx_b, idx_c, far


def _rope_tables(seq):
    half = MLA_ROPE // 2
    inv = ROPE_THETA ** (-jnp.arange(0, MLA_ROPE, 2, dtype=F32) / MLA_ROPE)
    ang = jnp.arange(seq, dtype=F32)[:, None] * inv[None, :]
    cos, sin = jnp.cos(ang), jnp.sin(ang)
    z = lambda w: jnp.zeros((seq, w), F32)
    cos_t = jnp.concatenate([z(MLA_NOPE), cos, cos, z(LANES - MLA_NOPE - MLA_ROPE)], axis=1)
    sin_lo = jnp.concatenate([z(MLA_NOPE), -sin, z(LANES - MLA_NOPE - half)], axis=1)
    sin_hi = jnp.concatenate([z(MLA_NOPE + half), sin, z(LANES - MLA_NOPE - MLA_ROPE)], axis=1)
    return cos_t, sin_lo, sin_hi


def _dup_heads(w, heads, dim):
    w4 = w.reshape(w.shape[:2] + (heads, dim))
    return jnp.concatenate([w4, w4], axis=-1).reshape(w.shape[:2] + (heads * 2 * dim,))


def kernel(x, p, ffn1_norm, ffn1_w_gate, ffn1_w_up, ffn1_w_down, mix_norm, w_in, mla_q_norm, mla_w_uq, mla_kv_norm, mla_w_ukv, swa_sink, diff_lambda_q1, diff_lambda_k1, diff_lambda_q2, diff_lambda_k2, diff_subln, rel_table, w_out, ffn2_norm, ffn2_w_gate, ffn2_w_up, ffn2_w_down, ple_norm, ple_w_gate, ple_w_proj, final_norm):
    b, s, d = x.shape
    depth = w_in.shape[0]
    n = b * s
    bf = lambda a: a.astype(BF16)
    vec = lambda a: a.astype(F32)[:, None, :]

    o = np.cumsum((0, MLA_Q_LORA, MLA_KV_LORA, MLA_ROPE, 1024, 256, 256, 1024, 1024, 1024, 3072))
    w_in_bf = bf(w_in)
    col = lambda i: w_in_bf[:, :, o[i]:o[i + 1]]
    zeros = lambda w: jnp.zeros((depth, d, w), BF16)
    wlat = bf(jnp.concatenate([col(0), col(1), zeros(MLA_NOPE), col(2),
                               zeros(LANES - MLA_NOPE - MLA_ROPE)], axis=-1))
    by = lambda w, groups: w.reshape(depth, d, groups, w.shape[-1] // groups)
    wb = bf(jnp.concatenate([by(col(3), SWA_KV_HEADS),
                             by(_dup_heads(col(4), SWA_KV_HEADS, SWA_HEAD_DIM), SWA_KV_HEADS),
                             by(_dup_heads(col(5), SWA_KV_HEADS, SWA_HEAD_DIM), SWA_KV_HEADS)],
                            axis=-1).reshape(depth, d, SWA_KV_HEADS * SWA_STEP_COLS))
    wc = bf(jnp.concatenate([by(col(6), DIFF_HEADS), by(col(7), DIFF_HEADS), by(col(8), DIFF_HEADS)],
                            axis=-1).reshape(depth, d, DIFF_HEADS * DIFF_STEP_COLS))
    q_scale = lambda dim, q_cols, step_cols, steps: jnp.tile(jnp.concatenate(
        [jnp.full((q_cols,), dim ** -0.5 * LOG2E, F32), jnp.ones((step_cols - q_cols,), F32)]),
        steps)[None, :]
    sb = q_scale(SWA_HEAD_DIM, 2 * LANES, SWA_STEP_COLS, SWA_KV_HEADS)
    sc = q_scale(DIFF_HEAD_DIM, LANES, DIFF_STEP_COLS, DIFF_HEADS)
    wgate = bf(col(9))
    uq = mla_w_uq.reshape(depth, MLA_Q_LORA, MLA_HEADS, MLA_NOPE + MLA_ROPE)
    wuq = bf(jnp.pad(uq, ((0, 0),) * 3 + ((0, LANES - MLA_NOPE - MLA_ROPE),))
             .reshape(depth, MLA_Q_LORA, MLA_HEADS * LANES))
    ukv = mla_w_ukv.reshape(depth, MLA_KV_LORA, MLA_HEADS, MLA_NOPE + MLA_V)
    wuk = bf(jnp.pad(ukv[..., :MLA_NOPE], ((0, 0),) * 3 + ((0, LANES - MLA_NOPE),))
             .reshape(depth, MLA_KV_LORA, MLA_HEADS * LANES))
    wuv = bf(ukv[..., MLA_NOPE:].reshape(depth, MLA_KV_LORA, MLA_HEADS * MLA_V))

    cos_t, sin_lo, sin_hi = _rope_tables(s)
    idx_b, idx_c, far_buckets = _bias_indices()
    n_bias_heads = SWA_HEADS + DIFF_HEADS
    table_flat = rel_table.astype(F32).reshape(REL_BUCKETS * n_bias_heads)
    per_kv = SWA_HEADS // SWA_KV_HEADS
    bias_b = _bias_tiles(table_flat, jnp.asarray(idx_b), 0, SWA_HEADS, n_bias_heads).reshape(
        SWA_KV_HEADS, per_kv * Q_BLOCK, 3 * Q_BLOCK)
    near_c = _bias_tiles(table_flat, jnp.asarray(idx_c), SWA_HEADS, DIFF_HEADS, n_bias_heads)
    far_c = (jnp.stack([rel_table[int(far_buckets[0]), SWA_HEADS:],
                        rel_table[int(far_buckets[1]), SWA_HEADS:]], axis=1).astype(F32)
             * LOG2E).reshape(2 * DIFF_HEADS)
    sink = swa_sink.astype(F32) * LOG2E
    lam_vecs = jnp.stack([diff_lambda_q1, diff_lambda_k1, diff_lambda_q2, diff_lambda_k2],
                         axis=1).astype(F32)
    f1 = (vec(ffn1_norm), bf(ffn1_w_gate), bf(ffn1_w_up), bf(ffn1_w_down))
    f2 = (vec(ffn2_norm), bf(ffn2_w_gate), bf(ffn2_w_up), bf(ffn2_w_down))
    mixn, qn, kvn, subln, plen = (vec(mix_norm), vec(mla_q_norm), vec(mla_kv_norm),
                                  vec(diff_subln), vec(ple_norm))
    wo, wpg, wpp = bf(w_out), bf(ple_w_gate), bf(ple_w_proj)
    fin = final_norm.astype(F32)[None, :]
    p2 = p.reshape(depth, n, PLE_DIM)

    h = x.reshape(n, d).astype(F32)
    for i in range(depth):
        h = _ffn(h, i, *f1)
        qkv_a, qkv_b, qkv_c, sg = _proj(h, i, s, mixn, wlat, wb, wc, wgate, qn, kvn, wuq, wuk, wuv,
                                        sb, sc, cos_t, sin_lo, sin_hi)
        r3 = lambda a: a.reshape(b, s, a.shape[-1])
        oa = _mla(r3(qkv_a))
        ob = _swa(sink[i], r3(qkv_b), bias_b)
        lambda_init = 0.8 - 0.6 * math.exp(-0.3 * i)
        oc = _diff(i, lambda_init, lam_vecs, far_c, r3(qkv_c), near_c, subln)
        h = _post(h, i, sg, oa.reshape(n, d), ob.reshape(n, d), oc.reshape(n, d), wo, *f2,
                  p2, plen, wpg, wpp, fin, final=(i == depth - 1))
    return h.reshape(b, s, d).astype(x.dtype)
```

```python
import functools
import math

import jax
import jax.numpy as jnp
import numpy as np
from jax import lax
from jax.experimental import pallas as pl
from jax.experimental.pallas import tpu as pltpu

F32 = jnp.float32
BF16 = jnp.bfloat16

D_MODEL = 1024
D_FF = 2816
PLE_DIM = 256
NORM_EPS = 1e-6
NEG_INF = -1e30
ROPE_THETA = 10000.0
MLA_HEADS = 16
MLA_NOPE = 64
MLA_ROPE = 32
MLA_V = 64
MLA_Q_LORA = 384
MLA_KV_LORA = 256
SWA_HEADS = 16
SWA_KV_HEADS = 4
SWA_HEAD_DIM = 64
WINDOW = 128
Q_BLOCK = 128
DIFF_HEADS = 8
DIFF_HEAD_DIM = 64
REL_BUCKETS = 32
REL_MAX_DIST = 128
LOG2E = math.log2(math.e)

LANES = 128
VMEM_LIMIT = 56 * 1024 * 1024

FFN_ROWS = 1024
PROJ_ROWS = 256
POST_ROWS = 512
MLA_Q_ROWS = 128
DIFF_Q_ROWS = 128
MLA_SCORE_LEAD = 4
DIFF_SCORE_LEAD = 2
MLA_STEP_COLS = 5 * LANES
SWA_STEP_COLS = 4 * LANES
DIFF_STEP_COLS = 3 * LANES


def _params(n_axes):
    return pltpu.CompilerParams(dimension_semantics=("arbitrary",) * n_axes,
                                vmem_limit_bytes=VMEM_LIMIT)


def _resident(shape, index_map):
    return pl.BlockSpec(shape, index_map, pipeline_mode=pl.Buffered(1))


def _rms(x, gain):
    return x * lax.rsqrt(jnp.mean(x * x, axis=-1, keepdims=True) + NORM_EPS) * gain


def _dot(a, b):
    return jnp.dot(a, b, preferred_element_type=F32)


def _dot_nt(a, b):
    return lax.dot_general(a, b, (((1,), (1,)), ((), ())), preferred_element_type=F32)


def _row_halves(n_rows):
    return [slice(i * (n_rows // 2), (i + 1) * (n_rows // 2)) for i in range(2)]


def _swiglu_half_step(hs, gain_ref, wg_ref, wu_ref, wd_ref):
    ns = [_rms(h, gain_ref[...]).astype(BF16) for h in hs]
    gus = [(_dot(n, wg_ref[...]), _dot(n, wu_ref[...])) for n in ns]
    acts = [(g * jax.nn.sigmoid(g) * u).astype(BF16) for g, u in gus]
    return [h + 0.5 * _dot(a, wd_ref[...]) for h, a in zip(hs, acts)]


def _ffn_body(h_ref, gain_ref, wg_ref, wu_ref, wd_ref, o_ref):
    halves = _row_halves(h_ref.shape[0])
    outs = _swiglu_half_step([h_ref[r, :] for r in halves], gain_ref, wg_ref, wu_ref, wd_ref)
    for r, o in zip(halves, outs):
        o_ref[r, :] = o


def _ffn(h, layer, gain, wg, wu, wd):
    n = h.shape[0]
    tm = FFN_ROWS
    row = lambda r: (r, 0)
    lay3 = lambda r: (layer, 0, 0)
    return pl.pallas_call(
        _ffn_body,
        grid=(n // tm,),
        in_specs=[pl.BlockSpec((tm, D_MODEL), row),
                  _resident((None, 1, D_MODEL), lay3),
                  _resident((None, D_MODEL, D_FF), lay3),
                  _resident((None, D_MODEL, D_FF), lay3),
                  _resident((None, D_FF, D_MODEL), lay3)],
        out_specs=pl.BlockSpec((tm, D_MODEL), row),
        out_shape=jax.ShapeDtypeStruct(h.shape, F32),
        compiler_params=_params(1),
        name="ffn",
    )(h, gain, wg, wu, wd)


def _rope(x, cos_t, sin_lo, sin_hi):
    return (x * cos_t + pltpu.roll(x, LANES - MLA_ROPE // 2, 1) * sin_lo
            + pltpu.roll(x, MLA_ROPE // 2, 1) * sin_hi)


def _proj_body(h_ref, gain_ref, wlat_ref, wb_ref, wc_ref, wgate_ref, qn_ref, kvn_ref,
               wuq_ref, wuk_ref, wuv_ref, sb_ref, sc_ref, cos_ref, slo_ref, shi_ref,
               a_ref, b_ref, c_ref, sg_ref):
    u = _rms(h_ref[...], gain_ref[...]).astype(BF16)

    lat = _dot(u, wlat_ref[...])
    b_ref[...] = (_dot(u, wb_ref[...]) * sb_ref[...]).astype(BF16)
    cq = _rms(lat[:, :MLA_Q_LORA], qn_ref[...]).astype(BF16)
    ckv = _rms(lat[:, MLA_Q_LORA:MLA_Q_LORA + MLA_KV_LORA], kvn_ref[...]).astype(BF16)
    kr = lat[:, MLA_Q_LORA + MLA_KV_LORA:]

    cos_t, sin_lo, sin_hi = cos_ref[...], slo_ref[...], shi_ref[...]
    lane = lax.broadcasted_iota(jnp.int32, cos_t.shape, 1)
    scale = (MLA_NOPE + MLA_ROPE) ** -0.5 * LOG2E
    q_cos = (cos_t + jnp.where(lane < MLA_NOPE, 1.0, 0.0)) * scale
    q_lo, q_hi = sin_lo * scale, sin_hi * scale
    kr_rot = _rope(kr, cos_t, sin_lo, sin_hi)

    q = _dot(cq, wuq_ref[...])
    kn = _dot(ckv, wuk_ref[...])
    v = _dot(ckv, wuv_ref[...])
    for hd in range(MLA_HEADS):
        sl = slice(hd * LANES, (hd + 1) * LANES)
        base = (hd // 2) * MLA_STEP_COLS + (hd % 2) * LANES
        a_ref[:, base:base + LANES] = _rope(q[:, sl], q_cos, q_lo, q_hi).astype(BF16)
        a_ref[:, base + 2 * LANES:base + 3 * LANES] = (kn[:, sl] + kr_rot).astype(BF16)
    for pair in range(MLA_HEADS // 2):
        base = pair * MLA_STEP_COLS + 4 * LANES
        a_ref[:, base:base + LANES] = v[:, pair * LANES:(pair + 1) * LANES].astype(BF16)

    c_ref[...] = (_dot(u, wc_ref[...]) * sc_ref[...]).astype(BF16)
    sg_ref[...] = jax.nn.sigmoid(_dot(u, wgate_ref[...])).astype(BF16)


def _proj(h, layer, seq, gain, wlat, wb, wc, wgate, qn, kvn, wuq, wuk, wuv, sb, sc,
          cos_t, sin_lo, sin_hi):
    n = h.shape[0]
    tm = PROJ_ROWS
    row = lambda r: (r, 0)
    lay3 = lambda r: (layer, 0, 0)
    pos = lambda r: (r % (seq // tm), 0)
    widths = (MLA_HEADS // 2 * MLA_STEP_COLS, SWA_KV_HEADS * SWA_STEP_COLS,
              DIFF_HEADS * DIFF_STEP_COLS, 3 * D_MODEL)
    res = lambda a: _resident((None,) + a.shape[1:], lay3)
    const = lambda a: _resident(a.shape, lambda r: (0, 0))
    return pl.pallas_call(
        _proj_body,
        grid=(n // tm,),
        in_specs=[pl.BlockSpec((tm, D_MODEL), row), res(gain), res(wlat), res(wb), res(wc), res(wgate),
                  res(qn), res(kvn), res(wuq), res(wuk), res(wuv), const(sb), const(sc),
                  pl.BlockSpec((tm, LANES), pos), pl.BlockSpec((tm, LANES), pos),
                  pl.BlockSpec((tm, LANES), pos)],
        out_specs=[pl.BlockSpec((tm, w), row) for w in widths],
        out_shape=[jax.ShapeDtypeStruct((n, w), BF16) for w in widths],
        compiler_params=_params(1),
        name="proj",
    )(h, gain, wlat, wb, wc, wgate, qn, kvn, wuq, wuk, wuv, sb, sc, cos_t, sin_lo, sin_hi)


def _low_half(shape):
    return lax.broadcasted_iota(jnp.int32, shape, len(shape) - 1) < LANES // 2


def _half(x, low, t):
    return jnp.where(low if t == 0 else jnp.logical_not(low), x, jnp.zeros_like(x))


def _with_ones(vx_ref, v):
    vx_ref[:, :LANES] = v
    vx_ref[:, LANES:] = jnp.ones((v.shape[0], LANES), BF16)


def _skewed(n_units, lead, score_unit, value_unit):
    for u in range(n_units + lead):
        if u < n_units:
            score_unit(u)
        if u >= lead:
            value_unit(u - lead)


def _mla_body(x_ref, o_ref, vx_ref):
    _with_ones(vx_ref, x_ref[:, 4 * LANES:])
    vx = vx_ref[...]
    tq = MLA_Q_ROWS
    low = _low_half((tq, LANES))
    scores, outs = {}, {}

    def score_unit(u):
        qi, j = divmod(u, 2)
        q = x_ref[qi * tq:(qi + 1) * tq, j * LANES:(j + 1) * LANES]
        scores[u] = _dot_nt(q, x_ref[:, (2 + j) * LANES:(3 + j) * LANES])

    def value_unit(u):
        qi, j = divmod(u, 2)
        s = scores.pop(u)
        p = jnp.exp2(s - jnp.max(s, axis=-1, keepdims=True)).astype(BF16)
        ol = _dot(p, vx)
        outs[j] = ol[:, :LANES] / ol[:, LANES:]
        if j == 1:
            o_ref[qi * tq:(qi + 1) * tq, :] = jnp.where(low, outs[0], outs[1]).astype(BF16)

    _skewed(2 * (x_ref.shape[0] // tq), MLA_SCORE_LEAD, score_unit, value_unit)


def _mla(qkv):
    b, s, _ = qkv.shape
    pair = lambda bi, hp: (bi, 0, hp)
    return pl.pallas_call(
        _mla_body,
        grid=(b, MLA_HEADS // 2),
        in_specs=[pl.BlockSpec((None, s, MLA_STEP_COLS), pair)],
        out_specs=pl.BlockSpec((None, s, LANES), pair),
        out_shape=jax.ShapeDtypeStruct((b, s, MLA_HEADS * MLA_V), BF16),
        scratch_shapes=[pltpu.VMEM((s, 2 * LANES), BF16)],
        compiler_params=_params(2),
        name="mla",
    )(qkv)


def _swa_body(sink_ref, x_ref, bias_ref, o_ref, vx_ref):
    group = pl.program_id(1)
    per_kv = SWA_HEADS // SWA_KV_HEADS
    qb = Q_BLOCK
    n_blocks = x_ref.shape[0] // qb
    _with_ones(vx_ref, x_ref[:, 3 * LANES:])
    low = _low_half((qb, LANES))
    sink = jnp.concatenate([jnp.full((qb, LANES), sink_ref[per_kv * group + t], F32)
                            for t in range(per_kv)], axis=0)
    for j in range(n_blocks):
        rows = slice(j * qb, (j + 1) * qb)
        qa, qc = x_ref[rows, :LANES], x_ref[rows, LANES:2 * LANES]
        lhs = jnp.concatenate([_half(qa, low, 0), _half(qa, low, 1),
                               _half(qc, low, 0), _half(qc, low, 1)], axis=0)
        lo = max(j - 1, 0) * qb
        hi = min(j + 2, n_blocks) * qb
        c0 = lo - (j - 1) * qb
        s = _dot_nt(lhs, x_ref[lo:hi, 2 * LANES:3 * LANES]) + bias_ref[:, c0:c0 + (hi - lo)]
        m = jnp.maximum(jnp.max(s, axis=-1, keepdims=True), sink)
        p = jnp.exp2(s - jnp.concatenate([m] * ((hi - lo) // LANES), axis=1)).astype(BF16)
        ol = _dot(p, vx_ref[lo:hi, :])
        o = ol[:, :LANES] / (ol[:, LANES:] + jnp.exp2(sink - m))
        o_ref[rows, :LANES] = jnp.where(low, o[:qb], o[qb:2 * qb]).astype(BF16)
        o_ref[rows, LANES:] = jnp.where(low, o[2 * qb:3 * qb], o[3 * qb:]).astype(BF16)


def _swa(sink, qkv, bias):
    b, s, _ = qkv.shape
    per_kv = SWA_HEADS // SWA_KV_HEADS
    return pl.pallas_call(
        _swa_body,
        grid=(b, SWA_KV_HEADS),
        in_specs=[pl.BlockSpec(memory_space=pltpu.SMEM),
                  pl.BlockSpec((None, s, SWA_STEP_COLS), lambda bi, g: (bi, 0, g)),
                  pl.BlockSpec((None, per_kv * Q_BLOCK, 3 * Q_BLOCK), lambda bi, g: (g, 0, 0))],
        out_specs=pl.BlockSpec((None, s, 2 * LANES), lambda bi, g: (bi, 0, g)),
        out_shape=jax.ShapeDtypeStruct((b, s, SWA_HEADS * SWA_HEAD_DIM), BF16),
        scratch_shapes=[pltpu.VMEM((s, 2 * LANES), BF16)],
        compiler_params=_params(2),
        name="swa",
    )(sink, qkv, bias)


def _diff_body(lam_ref, far_ref, x_ref, near_ref, subln_ref, o_ref, vx_ref, *, lambda_init):
    head = pl.program_id(0)
    seq = x_ref.shape[0]
    tq = DIFF_Q_ROWS
    lv = lam_ref[...]
    lam = (jnp.exp(jnp.sum(lv[0:1] * lv[1:2], axis=-1, keepdims=True))
           - jnp.exp(jnp.sum(lv[2:3] * lv[3:4], axis=-1, keepdims=True)) + lambda_init)
    c_left, c_right = far_ref[2 * head], far_ref[2 * head + 1]
    low = _low_half((tq, LANES))
    k = x_ref[:, LANES:2 * LANES]
    _with_ones(vx_ref, x_ref[:, 2 * LANES:])
    vx = vx_ref[...]
    scores, maps = {}, {}

    def score_unit(u):
        qt, t = divmod(u, 2)
        q = x_ref[qt * tq:(qt + 1) * tq, :LANES]
        scores[u] = _dot_nt(_half(q, low, t), k)

    def value_unit(u):
        qt, t = divmod(u, 2)
        s = scores.pop(u)
        a = max(0, qt - 1) * tq
        b = min(seq // tq, qt + 2) * tq
        t0 = a - (qt - 1) * tq
        near = s[:, a:b] + near_ref[:, t0:t0 + (b - a)]
        m = jnp.max(near, axis=-1, keepdims=True)
        if a > 0:
            m = jnp.maximum(m, jnp.max(s[:, :a], axis=-1, keepdims=True) + c_left)
        if b < seq:
            m = jnp.maximum(m, jnp.max(s[:, b:], axis=-1, keepdims=True) + c_right)
        parts = []
        if a > 0:
            parts.append(jnp.exp2(s[:, :a] - (m - c_left)).astype(BF16))
        parts.append(jnp.exp2(near - m).astype(BF16))
        if b < seq:
            parts.append(jnp.exp2(s[:, b:] - (m - c_right)).astype(BF16))
        ol = _dot(jnp.concatenate(parts, axis=1), vx)
        maps[t] = ol[:, :LANES] / ol[:, LANES:]
        if t == 1:
            o = maps[0] - lam * maps[1]
            o_ref[qt * tq:(qt + 1) * tq, :] = (_rms(o, subln_ref[...])
                                               * (1.0 - lambda_init)).astype(BF16)

    _skewed(2 * (seq // tq), DIFF_SCORE_LEAD, score_unit, value_unit)


def _diff(layer, lambda_init, lam_vecs, far, qkv, near, subln):
    b, s, _ = qkv.shape
    head_block = lambda hd, bi: (bi, 0, hd)
    return pl.pallas_call(
        functools.partial(_diff_body, lambda_init=lambda_init),
        grid=(DIFF_HEADS, b),
        in_specs=[pl.BlockSpec((None, 4, DIFF_HEAD_DIM), lambda hd, bi: (layer, 0, 0)),
                  pl.BlockSpec(memory_space=pltpu.SMEM),
                  pl.BlockSpec((None, s, DIFF_STEP_COLS), head_block),
                  pl.BlockSpec((None, DIFF_Q_ROWS, 3 * DIFF_Q_ROWS), lambda hd, bi: (hd, 0, 0)),
                  pl.BlockSpec((None, 1, LANES), lambda hd, bi: (layer, 0, 0))],
        out_specs=pl.BlockSpec((None, s, LANES), head_block),
        out_shape=jax.ShapeDtypeStruct((b, s, DIFF_HEADS * 2 * DIFF_HEAD_DIM), BF16),
        scratch_shapes=[pltpu.VMEM((s, 2 * LANES), BF16)],
        compiler_params=_params(2),
        name="diff",
    )(lam_vecs, far, qkv, near, subln)


def _post_body(h_ref, sg_ref, oa_ref, ob_ref, oc_ref, wo_ref, gain_ref, wg_ref, wu_ref, wd_ref,
               p_ref, pgain_ref, wpg_ref, wpp_ref, fin_ref, o_ref, *, final):
    d = D_MODEL
    halves = _row_halves(h_ref.shape[0])
    merged = [(sg_ref[r, :d].astype(F32) * oa_ref[r, :].astype(F32)
               + sg_ref[r, d:2 * d].astype(F32) * ob_ref[r, :].astype(F32)
               + sg_ref[r, 2 * d:].astype(F32) * oc_ref[r, :].astype(F32)).astype(BF16)
              for r in halves]
    hs = [h_ref[r, :] + _dot(m, wo_ref[...]) for r, m in zip(halves, merged)]
    hs = _swiglu_half_step(hs, gain_ref, wg_ref, wu_ref, wd_ref)
    gates = [jax.nn.sigmoid(_dot(_rms(h, pgain_ref[...]).astype(BF16), wpg_ref[...])) for h in hs]
    for r, h, gate in zip(halves, hs, gates):
        h = h + gate * _dot(p_ref[r, :].astype(BF16), wpp_ref[...])
        o_ref[r, :] = _rms(h, fin_ref[...]) if final else h


def _post(h, layer, sg, oa, ob, oc, w_out, gain, wg, wu, wd, p, pgain, wpg, wpp, final_norm, final):
    n = h.shape[0]
    tm = POST_ROWS
    row = lambda r: (r, 0)
    lay3 = lambda r: (layer, 0, 0)
    res = lambda a: _resident((None,) + a.shape[1:], lay3)
    return pl.pallas_call(
        functools.partial(_post_body, final=final),
        grid=(n // tm,),
        in_specs=[pl.BlockSpec((tm, D_MODEL), row), pl.BlockSpec((tm, 3 * D_MODEL), row),
                  pl.BlockSpec((tm, D_MODEL), row), pl.BlockSpec((tm, D_MODEL), row),
                  pl.BlockSpec((tm, D_MODEL), row),
                  res(w_out), res(gain), res(wg), res(wu), res(wd),
                  pl.BlockSpec((None, tm, PLE_DIM), lambda r: (layer, r, 0)),
                  res(pgain), res(wpg), res(wpp), _resident((1, D_MODEL), lambda r: (0, 0))],
        out_specs=pl.BlockSpec((tm, D_MODEL), row),
        out_shape=jax.ShapeDtypeStruct(h.shape, F32),
        compiler_params=_params(1),
        name="post",
    )(h, sg, oa, ob, oc, w_out, gain, wg, wu, wd, p, pgain, wpg, wpp, final_norm)


def _bias_body(table_ref, idx_ref, o_ref, *, first_head, n_heads):
    head = first_head + pl.program_id(0)
    idx = idx_ref[...]
    acc = jnp.full(idx.shape, NEG_INF, F32)
    for bucket in range(REL_BUCKETS):
        acc = jnp.where(idx == bucket, table_ref[bucket * n_heads + head] * LOG2E, acc)
    o_ref[...] = acc


def _bias_tiles(table_flat, idx, first_head, count, n_heads):
    return pl.pallas_call(
        functools.partial(_bias_body, first_head=first_head, n_heads=n_heads),
        grid=(count,),
        in_specs=[pl.BlockSpec(memory_space=pltpu.SMEM),
                  _resident(idx.shape, lambda hd: (0, 0))],
        out_specs=pl.BlockSpec((None,) + idx.shape, lambda hd: (hd, 0, 0)),
        out_shape=jax.ShapeDtypeStruct((count,) + idx.shape, F32),
        compiler_params=_params(1),
        name="bias",
    )(table_flat, idx)


def _t5_bucket_np(rel):
    nb = REL_BUCKETS // 2
    max_exact = nb // 2
    base = np.where(rel > 0, nb, 0)
    n = np.abs(rel)
    nf = np.maximum(n, 1).astype(np.float64)
    large = max_exact + (np.log(nf / max_exact) / math.log(REL_MAX_DIST / max_exact)
                         * (nb - max_exact)).astype(np.int64)
    large = np.minimum(large, nb - 1)
    return (base + np.where(n < max_exact, n, large)).astype(np.int32)


def _bias_indices():
    rel = (np.arange(3 * Q_BLOCK)[None, :] - WINDOW) - np.arange(Q_BLOCK)[:, None]
    idx_b = np.where(np.abs(rel) <= WINDOW, _t5_bucket_np(rel), REL_BUCKETS).astype(np.int32)
    t = DIFF_Q_ROWS
    rel = (np.arange(3 * t)[None, :] - t) - np.arange(t)[:, None]
    idx_c = _t5_bucket_np(rel)
    far = _t5_bucket_np(np.array([-t, t]))
    assert (far == _t5_bucket_np(np.array([-REL_MAX_DIST, REL_MAX_DIST]))).all()
    return idx_b, idx_c, far


def _rope_tables(seq):
    half = MLA_ROPE // 2
    inv = ROPE_THETA ** (-jnp.arange(0, MLA_ROPE, 2, dtype=F32) / MLA_ROPE)
    ang = jnp.arange(seq, dtype=F32)[:, None] * inv[None, :]
    cos, sin = jnp.cos(ang), jnp.sin(ang)
    z = lambda w: jnp.zeros((seq, w), F32)
    cos_t = jnp.concatenate([z(MLA_NOPE), cos, cos, z(LANES - MLA_NOPE - MLA_ROPE)], axis=1)
    sin_lo = jnp.concatenate([z(MLA_NOPE), -sin, z(LANES - MLA_NOPE - half)], axis=1)
    sin_hi = jnp.concatenate([z(MLA_NOPE + half), sin, z(LANES - MLA_NOPE - MLA_ROPE)], axis=1)
    return cos_t, sin_lo, sin_hi


def _dup_heads(w, heads, dim):
    w4 = w.reshape(w.shape[:2] + (heads, dim))
    return jnp.concatenate([w4, w4], axis=-1).reshape(w.shape[:2] + (heads * 2 * dim,))


def kernel(x, p, ffn1_norm, ffn1_w_gate, ffn1_w_up, ffn1_w_down, mix_norm, w_in, mla_q_norm, mla_w_uq, mla_kv_norm, mla_w_ukv, swa_sink, diff_lambda_q1, diff_lambda_k1, diff_lambda_q2, diff_lambda_k2, diff_subln, rel_table, w_out, ffn2_norm, ffn2_w_gate, ffn2_w_up, ffn2_w_down, ple_norm, ple_w_gate, ple_w_proj, final_norm):
    b, s, d = x.shape
    depth = w_in.shape[0]
    n = b * s
    bf = lambda a: a.astype(BF16)
    vec = lambda a: a.astype(F32)[:, None, :]

    o = np.cumsum((0, MLA_Q_LORA, MLA_KV_LORA, MLA_ROPE, 1024, 256, 256, 1024, 1024, 1024, 3072))
    w_in_bf = bf(w_in)
    col = lambda i: w_in_bf[:, :, o[i]:o[i + 1]]
    zeros = lambda w: jnp.zeros((depth, d, w), BF16)
    wlat = bf(jnp.concatenate([col(0), col(1), zeros(MLA_NOPE), col(2),
                               zeros(LANES - MLA_NOPE - MLA_ROPE)], axis=-1))
    by = lambda w, groups: w.reshape(depth, d, groups, w.shape[-1] // groups)
    wb = bf(jnp.concatenate([by(col(3), SWA_KV_HEADS),
                             by(_dup_heads(col(4), SWA_KV_HEADS, SWA_HEAD_DIM), SWA_KV_HEADS),
                             by(_dup_heads(col(5), SWA_KV_HEADS, SWA_HEAD_DIM), SWA_KV_HEADS)],
                            axis=-1).reshape(depth, d, SWA_KV_HEADS * SWA_STEP_COLS))
    wc = bf(jnp.concatenate([by(col(6), DIFF_HEADS), by(col(7), DIFF_HEADS), by(col(8), DIFF_HEADS)],
                            axis=-1).reshape(depth, d, DIFF_HEADS * DIFF_STEP_COLS))
    q_scale = lambda dim, q_cols, step_cols, steps: jnp.tile(jnp.concatenate(
        [jnp.full((q_cols,), dim ** -0.5 * LOG2E, F32), jnp.ones((step_cols - q_cols,), F32)]),
        steps)[None, :]
    sb = q_scale(SWA_HEAD_DIM, 2 * LANES, SWA_STEP_COLS, SWA_KV_HEADS)
    sc = q_scale(DIFF_HEAD_DIM, LANES, DIFF_STEP_COLS, DIFF_HEADS)
    wgate = bf(col(9))
    uq = mla_w_uq.reshape(depth, MLA_Q_LORA, MLA_HEADS, MLA_NOPE + MLA_ROPE)
    wuq = bf(jnp.pad(uq, ((0, 0),) * 3 + ((0, LANES - MLA_NOPE - MLA_ROPE),))
             .reshape(depth, MLA_Q_LORA, MLA_HEADS * LANES))
    ukv = mla_w_ukv.reshape(depth, MLA_KV_LORA, MLA_HEADS, MLA_NOPE + MLA_V)
    wuk = bf(jnp.pad(ukv[..., :MLA_NOPE], ((0, 0),) * 3 + ((0, LANES - MLA_NOPE),))
             .reshape(depth, MLA_KV_LORA, MLA_HEADS * LANES))
    wuv = bf(ukv[..., MLA_NOPE:].reshape(depth, MLA_KV_LORA, MLA_HEADS * MLA_V))

    cos_t, sin_lo, sin_hi = _rope_tables(s)
    idx_b, idx_c, far_buckets = _bias_indices()
    n_bias_heads = SWA_HEADS + DIFF_HEADS
    table_flat = rel_table.astype(F32).reshape(REL_BUCKETS * n_bias_heads)
    per_kv = SWA_HEADS // SWA_KV_HEADS
    bias_b = _bias_tiles(table_flat, jnp.asarray(idx_b), 0, SWA_HEADS, n_bias_heads).reshape(
        SWA_KV_HEADS, per_kv * Q_BLOCK, 3 * Q_BLOCK)
    near_c = _bias_tiles(table_flat, jnp.asarray(idx_c), SWA_HEADS, DIFF_HEADS, n_bias_heads)
    far_c = (jnp.stack([rel_table[int(far_buckets[0]), SWA_HEADS:],
                        rel_table[int(far_buckets[1]), SWA_HEADS:]], axis=1).astype(F32)
             * LOG2E).reshape(2 * DIFF_HEADS)
    sink = swa_sink.astype(F32) * LOG2E
    lam_vecs = jnp.stack([diff_lambda_q1, diff_lambda_k1, diff_lambda_q2, diff_lambda_k2],
                         axis=1).astype(F32)
    f1 = (vec(ffn1_norm), bf(ffn1_w_gate), bf(ffn1_w_up), bf(ffn1_w_down))
    f2 = (vec(ffn2_norm), bf(ffn2_w_gate), bf(ffn2_w_up), bf(ffn2_w_down))
    mixn, qn, kvn, subln, plen = (vec(mix_norm), vec(mla_q_norm), vec(mla_kv_norm),
                                  vec(diff_subln), vec(ple_norm))
    wo, wpg, wpp = bf(w_out), bf(ple_w_gate), bf(ple_w_proj)
    fin = final_norm.astype(F32)[None, :]
    p2 = p.reshape(depth, n, PLE_DIM)

    h = x.reshape(n, d).astype(F32)
    for i in range(depth):
        h = _ffn(h, i, *f1)
        qkv_a, qkv_b, qkv_c, sg = _proj(h, i, s, mixn, wlat, wb, wc, wgate, qn, kvn, wuq, wuk, wuv,
                                        sb, sc, cos_t, sin_lo, sin_hi)
        r3 = lambda a: a.reshape(b, s, a.shape[-1])
        oa = _mla(r3(qkv_a))
        ob = _swa(sink[i], r3(qkv_b), bias_b)
        lambda_init = 0.8 - 0.6 * math.exp(-0.3 * i)
        oc = _diff(i, lambda_init, lam_vecs, far_c, r3(qkv_c), near_c, subln)
        h = _post(h, i, sg, oa.reshape(n, d), ob.reshape(n, d), oc.reshape(n, d), wo, *f2,
                  p2, plen, wpg, wpp, fin, final=(i == depth - 1))
    return h.reshape(b, s, d).astype(x.dtype)
```

```python
import functools
import math

import jax
import jax.numpy as jnp
import numpy as np
from jax import lax
from jax.experimental import pallas as pl
from jax.experimental.pallas import tpu as pltpu

F32 = jnp.float32
BF16 = jnp.bfloat16

D_MODEL = 1024
D_FF = 2816
PLE_DIM = 256
NORM_EPS = 1e-6
NEG_INF = -1e30
ROPE_THETA = 10000.0
MLA_HEADS = 16
MLA_NOPE = 64
MLA_ROPE = 32
MLA_V = 64
MLA_Q_LORA = 384
MLA_KV_LORA = 256
SWA_HEADS = 16
SWA_KV_HEADS = 4
SWA_HEAD_DIM = 64
WINDOW = 128
Q_BLOCK = 128
DIFF_HEADS = 8
DIFF_HEAD_DIM = 64
REL_BUCKETS = 32
REL_MAX_DIST = 128
LOG2E = math.log2(math.e)

LANES = 128
VMEM_LIMIT = 56 * 1024 * 1024

FFN_ROWS = 1024
PROJ_ROWS = 256
POST_ROWS = 512
MLA_Q_ROWS = 128
DIFF_Q_ROWS = 128
MLA_SCORE_LEAD = 3
DIFF_SCORE_LEAD = 2
MLA_STEP_COLS = 5 * LANES
SWA_STEP_COLS = 4 * LANES
DIFF_STEP_COLS = 3 * LANES


def _params(n_axes):
    return pltpu.CompilerParams(dimension_semantics=("arbitrary",) * n_axes,
                                vmem_limit_bytes=VMEM_LIMIT)


def _resident(shape, index_map):
    return pl.BlockSpec(shape, index_map, pipeline_mode=pl.Buffered(1))


def _rms(x, gain):
    return x * lax.rsqrt(jnp.mean(x * x, axis=-1, keepdims=True) + NORM_EPS) * gain


def _dot(a, b):
    return jnp.dot(a, b, preferred_element_type=F32)


def _dot_nt(a, b):
    return lax.dot_general(a, b, (((1,), (1,)), ((), ())), preferred_element_type=F32)


def _row_halves(n_rows):
    return [slice(i * (n_rows // 2), (i + 1) * (n_rows // 2)) for i in range(2)]


def _swiglu_half_step(hs, gain_ref, wg_ref, wu_ref, wd_ref):
    ns = [_rms(h, gain_ref[...]).astype(BF16) for h in hs]
    gus = [(_dot(n, wg_ref[...]), _dot(n, wu_ref[...])) for n in ns]
    acts = [(g * jax.nn.sigmoid(g) * u).astype(BF16) for g, u in gus]
    return [h + 0.5 * _dot(a, wd_ref[...]) for h, a in zip(hs, acts)]


def _ffn_body(h_ref, gain_ref, wg_ref, wu_ref, wd_ref, o_ref):
    halves = _row_halves(h_ref.shape[0])
    outs = _swiglu_half_step([h_ref[r, :] for r in halves], gain_ref, wg_ref, wu_ref, wd_ref)
    for r, o in zip(halves, outs):
        o_ref[r, :] = o


def _ffn(h, layer, gain, wg, wu, wd):
    n = h.shape[0]
    tm = FFN_ROWS
    row = lambda r: (r, 0)
    lay3 = lambda r: (layer, 0, 0)
    return pl.pallas_call(
        _ffn_body,
        grid=(n // tm,),
        in_specs=[pl.BlockSpec((tm, D_MODEL), row),
                  _resident((None, 1, D_MODEL), lay3),
                  _resident((None, D_MODEL, D_FF), lay3),
                  _resident((None, D_MODEL, D_FF), lay3),
                  _resident((None, D_FF, D_MODEL), lay3)],
        out_specs=pl.BlockSpec((tm, D_MODEL), row),
        out_shape=jax.ShapeDtypeStruct(h.shape, F32),
        compiler_params=_params(1),
        name="ffn",
    )(h, gain, wg, wu, wd)


def _rope(x, cos_t, sin_lo, sin_hi):
    return (x * cos_t + pltpu.roll(x, LANES - MLA_ROPE // 2, 1) * sin_lo
            + pltpu.roll(x, MLA_ROPE // 2, 1) * sin_hi)


def _proj_body(h_ref, gain_ref, wlat_ref, wb_ref, wc_ref, wgate_ref, qn_ref, kvn_ref,
               wuq_ref, wuk_ref, wuv_ref, sb_ref, sc_ref, cos_ref, slo_ref, shi_ref,
               a_ref, b_ref, c_ref, sg_ref):
    u = _rms(h_ref[...], gain_ref[...]).astype(BF16)

    lat = _dot(u, wlat_ref[...])
    b_ref[...] = (_dot(u, wb_ref[...]) * sb_ref[...]).astype(BF16)
    sg_ref[...] = jax.nn.sigmoid(_dot(u, wgate_ref[...])).astype(BF16)
    cq = _rms(lat[:, :MLA_Q_LORA], qn_ref[...]).astype(BF16)
    ckv = _rms(lat[:, MLA_Q_LORA:MLA_Q_LORA + MLA_KV_LORA], kvn_ref[...]).astype(BF16)
    kr = lat[:, MLA_Q_LORA + MLA_KV_LORA:]

    cos_t, sin_lo, sin_hi = cos_ref[...], slo_ref[...], shi_ref[...]
    lane = lax.broadcasted_iota(jnp.int32, cos_t.shape, 1)
    scale = (MLA_NOPE + MLA_ROPE) ** -0.5 * LOG2E
    q_cos = (cos_t + jnp.where(lane < MLA_NOPE, 1.0, 0.0)) * scale
    q_lo, q_hi = sin_lo * scale, sin_hi * scale
    kr_rot = _rope(kr, cos_t, sin_lo, sin_hi)

    q = _dot(cq, wuq_ref[...])
    kn = _dot(ckv, wuk_ref[...])
    v = _dot(ckv, wuv_ref[...])
    for hd in range(MLA_HEADS):
        sl = slice(hd * LANES, (hd + 1) * LANES)
        base = (hd // 2) * MLA_STEP_COLS + (hd % 2) * LANES
        a_ref[:, base:base + LANES] = _rope(q[:, sl], q_cos, q_lo, q_hi).astype(BF16)
        a_ref[:, base + 2 * LANES:base + 3 * LANES] = (kn[:, sl] + kr_rot).astype(BF16)
    for pair in range(MLA_HEADS // 2):
        base = pair * MLA_STEP_COLS + 4 * LANES
        a_ref[:, base:base + LANES] = v[:, pair * LANES:(pair + 1) * LANES].astype(BF16)

    c_ref[...] = (_dot(u, wc_ref[...]) * sc_ref[...]).astype(BF16)


def _proj(h, layer, seq, gain, wlat, wb, wc, wgate, qn, kvn, wuq, wuk, wuv, sb, sc,
          cos_t, sin_lo, sin_hi):
    n = h.shape[0]
    tm = PROJ_ROWS
    row = lambda r: (r, 0)
    lay3 = lambda r: (layer, 0, 0)
    pos = lambda r: (r % (seq // tm), 0)
    widths = (MLA_HEADS // 2 * MLA_STEP_COLS, SWA_KV_HEADS * SWA_STEP_COLS,
              DIFF_HEADS * DIFF_STEP_COLS, 3 * D_MODEL)
    res = lambda a: _resident((None,) + a.shape[1:], lay3)
    const = lambda a: _resident(a.shape, lambda r: (0, 0))
    return pl.pallas_call(
        _proj_body,
        grid=(n // tm,),
        in_specs=[pl.BlockSpec((tm, D_MODEL), row), res(gain), res(wlat), res(wb), res(wc), res(wgate),
                  res(qn), res(kvn), res(wuq), res(wuk), res(wuv), const(sb), const(sc),
                  pl.BlockSpec((tm, LANES), pos), pl.BlockSpec((tm, LANES), pos),
                  pl.BlockSpec((tm, LANES), pos)],
        out_specs=[pl.BlockSpec((tm, w), row) for w in widths],
        out_shape=[jax.ShapeDtypeStruct((n, w), BF16) for w in widths],
        compiler_params=_params(1),
        name="proj",
    )(h, gain, wlat, wb, wc, wgate, qn, kvn, wuq, wuk, wuv, sb, sc, cos_t, sin_lo, sin_hi)


def _low_half(shape):
    return lax.broadcasted_iota(jnp.int32, shape, len(shape) - 1) < LANES // 2


def _half(x, low, t):
    return jnp.where(low if t == 0 else jnp.logical_not(low), x, jnp.zeros_like(x))


def _with_ones(vx_ref, v):
    vx_ref[:, :LANES] = v
    vx_ref[:, LANES:] = jnp.ones((v.shape[0], LANES), BF16)


def _skewed(n_units, lead, score_unit, value_unit):
    for u in range(n_units + lead):
        if u < n_units:
            score_unit(u)
        if u >= lead:
            value_unit(u - lead)


def _mla_body(x_ref, o_ref, vx_ref):
    _with_ones(vx_ref, x_ref[:, 4 * LANES:])
    vx = vx_ref[...]
    tq = MLA_Q_ROWS
    low = _low_half((tq, LANES))
    scores = {}

    def score_unit(qi):
        rows = slice(qi * tq, (qi + 1) * tq)
        scores[qi] = [_dot_nt(x_ref[rows, j * LANES:(j + 1) * LANES],
                              x_ref[:, (2 + j) * LANES:(3 + j) * LANES])
                      for j in range(2)]

    def value_unit(qi):
        p = jnp.concatenate([jnp.exp2(s - jnp.max(s, axis=-1, keepdims=True)).astype(BF16)
                             for s in scores.pop(qi)], axis=0)
        ol = _dot(p, vx)
        o = ol[:, :LANES] / ol[:, LANES:]
        o_ref[qi * tq:(qi + 1) * tq, :] = jnp.where(low, o[:tq], o[tq:]).astype(BF16)

    _skewed(x_ref.shape[0] // tq, MLA_SCORE_LEAD, score_unit, value_unit)


def _mla(qkv):
    b, s, _ = qkv.shape
    pair = lambda bi, hp: (bi, 0, hp)
    return pl.pallas_call(
        _mla_body,
        grid=(b, MLA_HEADS // 2),
        in_specs=[pl.BlockSpec((None, s, MLA_STEP_COLS), pair)],
        out_specs=pl.BlockSpec((None, s, LANES), pair),
        out_shape=jax.ShapeDtypeStruct((b, s, MLA_HEADS * MLA_V), BF16),
        scratch_shapes=[pltpu.VMEM((s, 2 * LANES), BF16)],
        compiler_params=_params(2),
        name="mla",
    )(qkv)


def _swa_body(sink_ref, x_ref, bias_ref, o_ref, vx_ref):
    group = pl.program_id(1)
    per_kv = SWA_HEADS // SWA_KV_HEADS
    qb = Q_BLOCK
    n_blocks = x_ref.shape[0] // qb
    _with_ones(vx_ref, x_ref[:, 3 * LANES:])
    low = _low_half((qb, LANES))
    sink = jnp.concatenate([jnp.full((qb, LANES), sink_ref[per_kv * group + t], F32)
                            for t in range(per_kv)], axis=0)
    for j in range(n_blocks):
        rows = slice(j * qb, (j + 1) * qb)
        qa, qc = x_ref[rows, :LANES], x_ref[rows, LANES:2 * LANES]
        lhs = jnp.concatenate([_half(qa, low, 0), _half(qa, low, 1),
                               _half(qc, low, 0), _half(qc, low, 1)], axis=0)
        lo = max(j - 1, 0) * qb
        hi = min(j + 2, n_blocks) * qb
        c0 = lo - (j - 1) * qb
        s = _dot_nt(lhs, x_ref[lo:hi, 2 * LANES:3 * LANES]) + bias_ref[:, c0:c0 + (hi - lo)]
        m = jnp.maximum(jnp.max(s, axis=-1, keepdims=True), sink)
        p = jnp.exp2(s - jnp.concatenate([m] * ((hi - lo) // LANES), axis=1)).astype(BF16)
        ol = _dot(p, vx_ref[lo:hi, :])
        o = ol[:, :LANES] / (ol[:, LANES:] + jnp.exp2(sink - m))
        o_ref[rows, :LANES] = jnp.where(low, o[:qb], o[qb:2 * qb]).astype(BF16)
        o_ref[rows, LANES:] = jnp.where(low, o[2 * qb:3 * qb], o[3 * qb:]).astype(BF16)


def _swa(sink, qkv, bias):
    b, s, _ = qkv.shape
    per_kv = SWA_HEADS // SWA_KV_HEADS
    return pl.pallas_call(
        _swa_body,
        grid=(b, SWA_KV_HEADS),
        in_specs=[pl.BlockSpec(memory_space=pltpu.SMEM),
                  pl.BlockSpec((None, s, SWA_STEP_COLS), lambda bi, g: (bi, 0, g)),
                  pl.BlockSpec((None, per_kv * Q_BLOCK, 3 * Q_BLOCK), lambda bi, g: (g, 0, 0))],
        out_specs=pl.BlockSpec((None, s, 2 * LANES), lambda bi, g: (bi, 0, g)),
        out_shape=jax.ShapeDtypeStruct((b, s, SWA_HEADS * SWA_HEAD_DIM), BF16),
        scratch_shapes=[pltpu.VMEM((s, 2 * LANES), BF16)],
        compiler_params=_params(2),
        name="swa",
    )(sink, qkv, bias)


def _diff_body(lam_ref, far_ref, x_ref, near_ref, subln_ref, o_ref, vx_ref, *, lambda_init):
    head = pl.program_id(0)
    seq = x_ref.shape[0]
    tq = DIFF_Q_ROWS
    lv = lam_ref[...]
    lam = (jnp.exp(jnp.sum(lv[0:1] * lv[1:2], axis=-1, keepdims=True))
           - jnp.exp(jnp.sum(lv[2:3] * lv[3:4], axis=-1, keepdims=True)) + lambda_init)
    c_left, c_right = far_ref[2 * head], far_ref[2 * head + 1]
    low = _low_half((tq, LANES))
    k = x_ref[:, LANES:2 * LANES]
    _with_ones(vx_ref, x_ref[:, 2 * LANES:])
    vx = vx_ref[...]
    scores = {}

    def score_unit(qt):
        q = x_ref[qt * tq:(qt + 1) * tq, :LANES]
        lhs = jnp.concatenate([_half(q, low, 0), _half(q, low, 1)], axis=0)
        scores[qt] = _dot_nt(lhs, k)

    def value_unit(qt):
        s = scores.pop(qt)
        a = max(0, qt - 1) * tq
        b = min(seq // tq, qt + 2) * tq
        t0 = a - (qt - 1) * tq
        bias = near_ref[:, t0:t0 + (b - a)]
        near = s[:, a:b] + jnp.concatenate([bias, bias], axis=0)
        m = jnp.max(near, axis=-1, keepdims=True)
        if a > 0:
            m = jnp.maximum(m, jnp.max(s[:, :a], axis=-1, keepdims=True) + c_left)
        if b < seq:
            m = jnp.maximum(m, jnp.max(s[:, b:], axis=-1, keepdims=True) + c_right)
        parts = []
        if a > 0:
            parts.append(jnp.exp2(s[:, :a] - (m - c_left)).astype(BF16))
        parts.append(jnp.exp2(near - m).astype(BF16))
        if b < seq:
            parts.append(jnp.exp2(s[:, b:] - (m - c_right)).astype(BF16))
        ol = _dot(jnp.concatenate(parts, axis=1), vx)
        on = ol[:, :LANES] / ol[:, LANES:]
        o = on[:tq] - lam * on[tq:]
        o_ref[qt * tq:(qt + 1) * tq, :] = (_rms(o, subln_ref[...])
                                           * (1.0 - lambda_init)).astype(BF16)

    _skewed(seq // tq, DIFF_SCORE_LEAD, score_unit, value_unit)


def _diff(layer, lambda_init, lam_vecs, far, qkv, near, subln):
    b, s, _ = qkv.shape
    head_block = lambda hd, bi: (bi, 0, hd)
    return pl.pallas_call(
        functools.partial(_diff_body, lambda_init=lambda_init),
        grid=(DIFF_HEADS, b),
        in_specs=[pl.BlockSpec((None, 4, DIFF_HEAD_DIM), lambda hd, bi: (layer, 0, 0)),
                  pl.BlockSpec(memory_space=pltpu.SMEM),
                  pl.BlockSpec((None, s, DIFF_STEP_COLS), head_block),
                  pl.BlockSpec((None, DIFF_Q_ROWS, 3 * DIFF_Q_ROWS), lambda hd, bi: (hd, 0, 0)),
                  pl.BlockSpec((None, 1, LANES), lambda hd, bi: (layer, 0, 0))],
        out_specs=pl.BlockSpec((None, s, LANES), head_block),
        out_shape=jax.ShapeDtypeStruct((b, s, DIFF_HEADS * 2 * DIFF_HEAD_DIM), BF16),
        scratch_shapes=[pltpu.VMEM((s, 2 * LANES), BF16)],
        compiler_params=_params(2),
        name="diff",
    )(lam_vecs, far, qkv, near, subln)


def _post_body(h_ref, sg_ref, oa_ref, ob_ref, oc_ref, wo_ref, gain_ref, wg_ref, wu_ref, wd_ref,
               p_ref, pgain_ref, wpg_ref, wpp_ref, fin_ref, o_ref, *, final):
    d = D_MODEL
    halves = _row_halves(h_ref.shape[0])
    merged = [(sg_ref[r, :d].astype(F32) * oa_ref[r, :].astype(F32)
               + sg_ref[r, d:2 * d].astype(F32) * ob_ref[r, :].astype(F32)
               + sg_ref[r, 2 * d:].astype(F32) * oc_ref[r, :].astype(F32)).astype(BF16)
              for r in halves]
    hs = [h_ref[r, :] + _dot(m, wo_ref[...]) for r, m in zip(halves, merged)]
    hs = _swiglu_half_step(hs, gain_ref, wg_ref, wu_ref, wd_ref)
    gates = [jax.nn.sigmoid(_dot(_rms(h, pgain_ref[...]).astype(BF16), wpg_ref[...])) for h in hs]
    for r, h, gate in zip(halves, hs, gates):
        h = h + gate * _dot(p_ref[r, :].astype(BF16), wpp_ref[...])
        o_ref[r, :] = _rms(h, fin_ref[...]) if final else h


def _post(h, layer, sg, oa, ob, oc, w_out, gain, wg, wu, wd, p, pgain, wpg, wpp, final_norm, final):
    n = h.shape[0]
    tm = POST_ROWS
    row = lambda r: (r, 0)
    lay3 = lambda r: (layer, 0, 0)
    res = lambda a: _resident((None,) + a.shape[1:], lay3)
    return pl.pallas_call(
        functools.partial(_post_body, final=final),
        grid=(n // tm,),
        in_specs=[pl.BlockSpec((tm, D_MODEL), row), pl.BlockSpec((tm, 3 * D_MODEL), row),
                  pl.BlockSpec((tm, D_MODEL), row), pl.BlockSpec((tm, D_MODEL), row),
                  pl.BlockSpec((tm, D_MODEL), row),
                  res(w_out), res(gain), res(wg), res(wu), res(wd),
                  pl.BlockSpec((None, tm, PLE_DIM), lambda r: (layer, r, 0)),
                  res(pgain), res(wpg), res(wpp), _resident((1, D_MODEL), lambda r: (0, 0))],
        out_specs=pl.BlockSpec((tm, D_MODEL), row),
        out_shape=jax.ShapeDtypeStruct(h.shape, F32),
        compiler_params=_params(1),
        name="post",
    )(h, sg, oa, ob, oc, w_out, gain, wg, wu, wd, p, pgain, wpg, wpp, final_norm)


def _bias_body(table_ref, idx_ref, o_ref, *, first_head, n_heads):
    head = first_head + pl.program_id(0)
    idx = idx_ref[...]
    acc = jnp.full(idx.shape, NEG_INF, F32)
    for bucket in range(REL_BUCKETS):
        acc = jnp.where(idx == bucket, table_ref[bucket * n_heads + head] * LOG2E, acc)
    o_ref[...] = acc


def _bias_tiles(table_flat, idx, first_head, count, n_heads):
    return pl.pallas_call(
        functools.partial(_bias_body, first_head=first_head, n_heads=n_heads),
        grid=(count,),
        in_specs=[pl.BlockSpec(memory_space=pltpu.SMEM),
                  _resident(idx.shape, lambda hd: (0, 0))],
        out_specs=pl.BlockSpec((None,) + idx.shape, lambda hd: (hd, 0, 0)),
        out_shape=jax.ShapeDtypeStruct((count,) + idx.shape, F32),
        compiler_params=_params(1),
        name="bias",
    )(table_flat, idx)


def _t5_bucket_np(rel):
    nb = REL_BUCKETS // 2
    max_exact = nb // 2
    base = np.where(rel > 0, nb, 0)
    n = np.abs(rel)
    nf = np.maximum(n, 1).astype(np.float64)
    large = max_exact + (np.log(nf / max_exact) / math.log(REL_MAX_DIST / max_exact)
                         * (nb - max_exact)).astype(np.int64)
    large = np.minimum(large, nb - 1)
    return (base + np.where(n < max_exact, n, large)).astype(np.int32)


def _bias_indices():
    rel = (np.arange(3 * Q_BLOCK)[None, :] - WINDOW) - np.arange(Q_BLOCK)[:, None]
    idx_b = np.where(np.abs(rel) <= WINDOW, _t5_bucket_np(rel), REL_BUCKETS).astype(np.int32)
    t = DIFF_Q_ROWS
    rel = (np.arange(3 * t)[None, :] - t) - np.arange(t)[:, None]
    idx_c = _t5_bucket_np(rel)
    far = _t5_bucket_np(np.array([-t, t]))
    assert (far == _t5_bucket_np(np.array([-REL_MAX_DIST, REL_MAX_DIST]))).all()
    return idx_b, idx_c, far


def _rope_tables(seq):
    half = MLA_ROPE // 2
    inv = ROPE_THETA ** (-jnp.arange(0, MLA_ROPE, 2, dtype=F32) / MLA_ROPE)
    ang = jnp.arange(seq, dtype=F32)[:, None] * inv[None, :]
    cos, sin = jnp.cos(ang), jnp.sin(ang)
    z = lambda w: jnp.zeros((seq, w), F32)
    cos_t = jnp.concatenate([z(MLA_NOPE), cos, cos, z(LANES - MLA_NOPE - MLA_ROPE)], axis=1)
    sin_lo = jnp.concatenate([z(MLA_NOPE), -sin, z(LANES - MLA_NOPE - half)], axis=1)
    sin_hi = jnp.concatenate([z(MLA_NOPE + half), sin, z(LANES - MLA_NOPE - MLA_ROPE)], axis=1)
    return cos_t, sin_lo, sin_hi


def _dup_heads(w, heads, dim):
    w4 = w.reshape(w.shape[:2] + (heads, dim))
    return jnp.concatenate([w4, w4], axis=-1).reshape(w.shape[:2] + (heads * 2 * dim,))


def kernel(x, p, ffn1_norm, ffn1_w_gate, ffn1_w_up, ffn1_w_down, mix_norm, w_in, mla_q_norm, mla_w_uq, mla_kv_norm, mla_w_ukv, swa_sink, diff_lambda_q1, diff_lambda_k1, diff_lambda_q2, diff_lambda_k2, diff_subln, rel_table, w_out, ffn2_norm, ffn2_w_gate, ffn2_w_up, ffn2_w_down, ple_norm, ple_w_gate, ple_w_proj, final_norm):
    b, s, d = x.shape
    depth = w_in.shape[0]
    n = b * s
    bf = lambda a: a.astype(BF16)
    vec = lambda a: a.astype(F32)[:, None, :]

    o = np.cumsum((0, MLA_Q_LORA, MLA_KV_LORA, MLA_ROPE, 1024, 256, 256, 1024, 1024, 1024, 3072))
    w_in_bf = bf(w_in)
    col = lambda i: w_in_bf[:, :, o[i]:o[i + 1]]
    zeros = lambda w: jnp.zeros((depth, d, w), BF16)
    wlat = bf(jnp.concatenate([col(0), col(1), zeros(MLA_NOPE), col(2),
                               zeros(LANES - MLA_NOPE - MLA_ROPE)], axis=-1))
    by = lambda w, groups: w.reshape(depth, d, groups, w.shape[-1] // groups)
    wb = bf(jnp.concatenate([by(col(3), SWA_KV_HEADS),
                             by(_dup_heads(col(4), SWA_KV_HEADS, SWA_HEAD_DIM), SWA_KV_HEADS),
                             by(_dup_heads(col(5), SWA_KV_HEADS, SWA_HEAD_DIM), SWA_KV_HEADS)],
                            axis=-1).reshape(depth, d, SWA_KV_HEADS * SWA_STEP_COLS))
    wc = bf(jnp.concatenate([by(col(6), DIFF_HEADS), by(col(7), DIFF_HEADS), by(col(8), DIFF_HEADS)],
                            axis=-1).reshape(depth, d, DIFF_HEADS * DIFF_STEP_COLS))
    q_scale = lambda dim, q_cols, step_cols, steps: jnp.tile(jnp.concatenate(
        [jnp.full((q_cols,), dim ** -0.5 * LOG2E, F32), jnp.ones((step_cols - q_cols,), F32)]),
        steps)[None, :]
    sb = q_scale(SWA_HEAD_DIM, 2 * LANES, SWA_STEP_COLS, SWA_KV_HEADS)
    sc = q_scale(DIFF_HEAD_DIM, LANES, DIFF_STEP_COLS, DIFF_HEADS)
    wgate = bf(col(9))
    uq = mla_w_uq.reshape(depth, MLA_Q_LORA, MLA_HEADS, MLA_NOPE + MLA_ROPE)
    wuq = bf(jnp.pad(uq, ((0, 0),) * 3 + ((0, LANES - MLA_NOPE - MLA_ROPE),))
             .reshape(depth, MLA_Q_LORA, MLA_HEADS * LANES))
    ukv = mla_w_ukv.reshape(depth, MLA_KV_LORA, MLA_HEADS, MLA_NOPE + MLA_V)
    wuk = bf(jnp.pad(ukv[..., :MLA_NOPE], ((0, 0),) * 3 + ((0, LANES - MLA_NOPE),))
             .reshape(depth, MLA_KV_LORA, MLA_HEADS * LANES))
    wuv = bf(ukv[..., MLA_NOPE:].reshape(depth, MLA_KV_LORA, MLA_HEADS * MLA_V))

    cos_t, sin_lo, sin_hi = _rope_tables(s)
    idx_b, idx_c, far_buckets = _bias_indices()
    n_bias_heads = SWA_HEADS + DIFF_HEADS
    table_flat = rel_table.astype(F32).reshape(REL_BUCKETS * n_bias_heads)
    per_kv = SWA_HEADS // SWA_KV_HEADS
    bias_b = _bias_tiles(table_flat, jnp.asarray(idx_b), 0, SWA_HEADS, n_bias_heads).reshape(
        SWA_KV_HEADS, per_kv * Q_BLOCK, 3 * Q_BLOCK)
    near_c = _bias_tiles(table_flat, jnp.asarray(idx_c), SWA_HEADS, DIFF_HEADS, n_bias_heads)
    far_c = (jnp.stack([rel_table[int(far_buckets[0]), SWA_HEADS:],
                        rel_table[int(far_buckets[1]), SWA_HEADS:]], axis=1).astype(F32)
             * LOG2E).reshape(2 * DIFF_HEADS)
    sink = swa_sink.astype(F32) * LOG2E
    lam_vecs = jnp.stack([diff_lambda_q1, diff_lambda_k1, diff_lambda_q2, diff_lambda_k2],
                         axis=1).astype(F32)
    f1 = (vec(ffn1_norm), bf(ffn1_w_gate), bf(ffn1_w_up), bf(ffn1_w_down))
    f2 = (vec(ffn2_norm), bf(ffn2_w_gate), bf(ffn2_w_up), bf(ffn2_w_down))
    mixn, qn, kvn, subln, plen = (vec(mix_norm), vec(mla_q_norm), vec(mla_kv_norm),
                                  vec(diff_subln), vec(ple_norm))
    wo, wpg, wpp = bf(w_out), bf(ple_w_gate), bf(ple_w_proj)
    fin = final_norm.astype(F32)[None, :]
    p2 = p.reshape(depth, n, PLE_DIM)

    h = x.reshape(n, d).astype(F32)
    for i in range(depth):
        h = _ffn(h, i, *f1)
        qkv_a, qkv_b, qkv_c, sg = _proj(h, i, s, mixn, wlat, wb, wc, wgate, qn, kvn, wuq, wuk, wuv,
                                        sb, sc, cos_t, sin_lo, sin_hi)
        r3 = lambda a: a.reshape(b, s, a.shape[-1])
        oa = _mla(r3(qkv_a))
        ob = _swa(sink[i], r3(qkv_b), bias_b)
        lambda_init = 0.8 - 0.6 * math.exp(-0.3 * i)
        oc = _diff(i, lambda_init, lam_vecs, far_c, r3(qkv_c), near_c, subln)
        h = _post(h, i, sg, oa.reshape(n, d), ob.reshape(n, d), oc.reshape(n, d), wo, *f2,
                  p2, plen, wpg, wpp, fin, final=(i == depth - 1))
    return h.reshape(b, s, d).astype(x.dtype)
```

```python
import functools
import math

import jax
import jax.numpy as jnp
import numpy as np
from jax import lax
from jax.experimental import pallas as pl
from jax.experimental.pallas import tpu as pltpu

F32 = jnp.float32
BF16 = jnp.bfloat16

D_MODEL = 1024
D_FF = 2816
PLE_DIM = 256
NORM_EPS = 1e-6
NEG_INF = -1e30
ROPE_THETA = 10000.0
MLA_HEADS = 16
MLA_NOPE = 64
MLA_ROPE = 32
MLA_V = 64
MLA_Q_LORA = 384
MLA_KV_LORA = 256
SWA_HEADS = 16
SWA_KV_HEADS = 4
SWA_HEAD_DIM = 64
WINDOW = 128
Q_BLOCK = 128
DIFF_HEADS = 8
DIFF_HEAD_DIM = 64
REL_BUCKETS = 32
REL_MAX_DIST = 128
LOG2E = math.log2(math.e)

LANES = 128
VMEM_LIMIT = 56 * 1024 * 1024

FFN_ROWS = 1024
PROJ_ROWS = 256
POST_ROWS = 512
MLA_Q_ROWS = 128
DIFF_Q_ROWS = 128
MLA_SCORE_TILES = 2
MLA_SCORE_LEAD = 2
DIFF_SCORE_LEAD = 2
MLA_STEP_COLS = 5 * LANES
SWA_STEP_COLS = 4 * LANES
DIFF_STEP_COLS = 3 * LANES


def _params(n_axes):
    return pltpu.CompilerParams(dimension_semantics=("arbitrary",) * n_axes,
                                vmem_limit_bytes=VMEM_LIMIT)


def _resident(shape, index_map):
    return pl.BlockSpec(shape, index_map, pipeline_mode=pl.Buffered(1))


def _rms(x, gain):
    return x * lax.rsqrt(jnp.mean(x * x, axis=-1, keepdims=True) + NORM_EPS) * gain


def _dot(a, b):
    return jnp.dot(a, b, preferred_element_type=F32)


def _dot_nt(a, b):
    return lax.dot_general(a, b, (((1,), (1,)), ((), ())), preferred_element_type=F32)


def _row_halves(n_rows):
    return [slice(i * (n_rows // 2), (i + 1) * (n_rows // 2)) for i in range(2)]


def _swiglu_half_step(hs, gain_ref, wg_ref, wu_ref, wd_ref):
    ns = [_rms(h, gain_ref[...]).astype(BF16) for h in hs]
    gus = [(_dot(n, wg_ref[...]), _dot(n, wu_ref[...])) for n in ns]
    acts = [(g * jax.nn.sigmoid(g) * u).astype(BF16) for g, u in gus]
    return [h + 0.5 * _dot(a, wd_ref[...]) for h, a in zip(hs, acts)]


def _ffn_body(h_ref, gain_ref, wg_ref, wu_ref, wd_ref, o_ref):
    halves = _row_halves(h_ref.shape[0])
    outs = _swiglu_half_step([h_ref[r, :] for r in halves], gain_ref, wg_ref, wu_ref, wd_ref)
    for r, o in zip(halves, outs):
        o_ref[r, :] = o


def _ffn(h, layer, gain, wg, wu, wd):
    n = h.shape[0]
    tm = FFN_ROWS
    row = lambda r: (r, 0)
    lay3 = lambda r: (layer, 0, 0)
    return pl.pallas_call(
        _ffn_body,
        grid=(n // tm,),
        in_specs=[pl.BlockSpec((tm, D_MODEL), row),
                  _resident((None, 1, D_MODEL), lay3),
                  _resident((None, D_MODEL, D_FF), lay3),
                  _resident((None, D_MODEL, D_FF), lay3),
                  _resident((None, D_FF, D_MODEL), lay3)],
        out_specs=pl.BlockSpec((tm, D_MODEL), row),
        out_shape=jax.ShapeDtypeStruct(h.shape, F32),
        compiler_params=_params(1),
        name="ffn",
    )(h, gain, wg, wu, wd)


def _rope(x, cos_t, sin_lo, sin_hi):
    return (x * cos_t + pltpu.roll(x, LANES - MLA_ROPE // 2, 1) * sin_lo
            + pltpu.roll(x, MLA_ROPE // 2, 1) * sin_hi)


def _proj_body(h_ref, gain_ref, wlat_ref, wb_ref, wc_ref, wgate_ref, qn_ref, kvn_ref,
               wuq_ref, wuk_ref, wuv_ref, sb_ref, sc_ref, cos_ref, slo_ref, shi_ref,
               a_ref, b_ref, c_ref, sg_ref):
    u = _rms(h_ref[...], gain_ref[...]).astype(BF16)

    lat = _dot(u, wlat_ref[...])
    b_ref[...] = (_dot(u, wb_ref[...]) * sb_ref[...]).astype(BF16)
    sg_ref[...] = jax.nn.sigmoid(_dot(u, wgate_ref[...])).astype(BF16)
    cq = _rms(lat[:, :MLA_Q_LORA], qn_ref[...]).astype(BF16)
    ckv = _rms(lat[:, MLA_Q_LORA:MLA_Q_LORA + MLA_KV_LORA], kvn_ref[...]).astype(BF16)
    kr = lat[:, MLA_Q_LORA + MLA_KV_LORA:]

    cos_t, sin_lo, sin_hi = cos_ref[...], slo_ref[...], shi_ref[...]
    lane = lax.broadcasted_iota(jnp.int32, cos_t.shape, 1)
    scale = (MLA_NOPE + MLA_ROPE) ** -0.5 * LOG2E
    q_cos = (cos_t + jnp.where(lane < MLA_NOPE, 1.0, 0.0)) * scale
    q_lo, q_hi = sin_lo * scale, sin_hi * scale
    kr_rot = _rope(kr, cos_t, sin_lo, sin_hi)

    q = _dot(cq, wuq_ref[...])
    kn = _dot(ckv, wuk_ref[...])
    v = _dot(ckv, wuv_ref[...])
    for hd in range(MLA_HEADS):
        sl = slice(hd * LANES, (hd + 1) * LANES)
        base = (hd // 2) * MLA_STEP_COLS + (hd % 2) * LANES
        a_ref[:, base:base + LANES] = _rope(q[:, sl], q_cos, q_lo, q_hi).astype(BF16)
        a_ref[:, base + 2 * LANES:base + 3 * LANES] = (kn[:, sl] + kr_rot).astype(BF16)
    for pair in range(MLA_HEADS // 2):
        base = pair * MLA_STEP_COLS + 4 * LANES
        a_ref[:, base:base + LANES] = v[:, pair * LANES:(pair + 1) * LANES].astype(BF16)

    c_ref[...] = (_dot(u, wc_ref[...]) * sc_ref[...]).astype(BF16)


def _proj(h, layer, seq, gain, wlat, wb, wc, wgate, qn, kvn, wuq, wuk, wuv, sb, sc,
          cos_t, sin_lo, sin_hi):
    n = h.shape[0]
    tm = PROJ_ROWS
    row = lambda r: (r, 0)
    lay3 = lambda r: (layer, 0, 0)
    pos = lambda r: (r % (seq // tm), 0)
    widths = (MLA_HEADS // 2 * MLA_STEP_COLS, SWA_KV_HEADS * SWA_STEP_COLS,
              DIFF_HEADS * DIFF_STEP_COLS, 3 * D_MODEL)
    res = lambda a: _resident((None,) + a.shape[1:], lay3)
    const = lambda a: _resident(a.shape, lambda r: (0, 0))
    return pl.pallas_call(
        _proj_body,
        grid=(n // tm,),
        in_specs=[pl.BlockSpec((tm, D_MODEL), row), res(gain), res(wlat), res(wb), res(wc), res(wgate),
                  res(qn), res(kvn), res(wuq), res(wuk), res(wuv), const(sb), const(sc),
                  pl.BlockSpec((tm, LANES), pos), pl.BlockSpec((tm, LANES), pos),
                  pl.BlockSpec((tm, LANES), pos)],
        out_specs=[pl.BlockSpec((tm, w), row) for w in widths],
        out_shape=[jax.ShapeDtypeStruct((n, w), BF16) for w in widths],
        compiler_params=_params(1),
        name="proj",
    )(h, gain, wlat, wb, wc, wgate, qn, kvn, wuq, wuk, wuv, sb, sc, cos_t, sin_lo, sin_hi)


def _low_half(shape):
    return lax.broadcasted_iota(jnp.int32, shape, len(shape) - 1) < LANES // 2


def _half(x, low, t):
    return jnp.where(low if t == 0 else jnp.logical_not(low), x, jnp.zeros_like(x))


def _with_ones(vx_ref, v):
    vx_ref[:, :LANES] = v
    vx_ref[:, LANES:] = jnp.ones((v.shape[0], LANES), BF16)


def _skewed(n_units, lead, score_unit, value_unit):
    for u in range(n_units + lead):
        if u < n_units:
            score_unit(u)
        if u >= lead:
            value_unit(u - lead)


def _mla_body(x_ref, o_ref, vx_ref):
    _with_ones(vx_ref, x_ref[:, 4 * LANES:])
    vx = vx_ref[...]
    tq = MLA_Q_ROWS
    low = _low_half((tq, LANES))
    scores = {}

    wide = MLA_SCORE_TILES

    def score_unit(g):
        rows = slice(g * wide * tq, (g + 1) * wide * tq)
        scores[g] = [_dot_nt(x_ref[rows, j * LANES:(j + 1) * LANES],
                             x_ref[:, (2 + j) * LANES:(3 + j) * LANES])
                     for j in range(2)]

    def value_unit(g):
        heads = scores.pop(g)
        for w in range(wide):
            r = slice(w * tq, (w + 1) * tq)
            p = jnp.concatenate([jnp.exp2(s[r] - jnp.max(s[r], axis=-1, keepdims=True)).astype(BF16)
                                 for s in heads], axis=0)
            ol = _dot(p, vx)
            o = ol[:, :LANES] / ol[:, LANES:]
            qi = g * wide + w
            o_ref[qi * tq:(qi + 1) * tq, :] = jnp.where(low, o[:tq], o[tq:]).astype(BF16)

    _skewed(x_ref.shape[0] // (wide * tq), MLA_SCORE_LEAD, score_unit, value_unit)


def _mla(qkv):
    b, s, _ = qkv.shape
    pair = lambda bi, hp: (bi, 0, hp)
    return pl.pallas_call(
        _mla_body,
        grid=(b, MLA_HEADS // 2),
        in_specs=[pl.BlockSpec((None, s, MLA_STEP_COLS), pair)],
        out_specs=pl.BlockSpec((None, s, LANES), pair),
        out_shape=jax.ShapeDtypeStruct((b, s, MLA_HEADS * MLA_V), BF16),
        scratch_shapes=[pltpu.VMEM((s, 2 * LANES), BF16)],
        compiler_params=_params(2),
        name="mla",
    )(qkv)


def _swa_body(sink_ref, x_ref, bias_ref, o_ref, vx_ref):
    group = pl.program_id(1)
    per_kv = SWA_HEADS // SWA_KV_HEADS
    qb = Q_BLOCK
    n_blocks = x_ref.shape[0] // qb
    _with_ones(vx_ref, x_ref[:, 3 * LANES:])
    low = _low_half((qb, LANES))
    sink = jnp.concatenate([jnp.full((qb, LANES), sink_ref[per_kv * group + t], F32)
                            for t in range(per_kv)], axis=0)
    for j in range(n_blocks):
        rows = slice(j * qb, (j + 1) * qb)
        qa, qc = x_ref[rows, :LANES], x_ref[rows, LANES:2 * LANES]
        lhs = jnp.concatenate([_half(qa, low, 0), _half(qa, low, 1),
                               _half(qc, low, 0), _half(qc, low, 1)], axis=0)
        lo = max(j - 1, 0) * qb
        hi = min(j + 2, n_blocks) * qb
        c0 = lo - (j - 1) * qb
        s = _dot_nt(lhs, x_ref[lo:hi, 2 * LANES:3 * LANES]) + bias_ref[:, c0:c0 + (hi - lo)]
        m = jnp.maximum(jnp.max(s, axis=-1, keepdims=True), sink)
        p = jnp.exp2(s - jnp.concatenate([m] * ((hi - lo) // LANES), axis=1)).astype(BF16)
        ol = _dot(p, vx_ref[lo:hi, :])
        o = ol[:, :LANES] / (ol[:, LANES:] + jnp.exp2(sink - m))
        o_ref[rows, :LANES] = jnp.where(low, o[:qb], o[qb:2 * qb]).astype(BF16)
        o_ref[rows, LANES:] = jnp.where(low, o[2 * qb:3 * qb], o[3 * qb:]).astype(BF16)


def _swa(sink, qkv, bias):
    b, s, _ = qkv.shape
    per_kv = SWA_HEADS // SWA_KV_HEADS
    return pl.pallas_call(
        _swa_body,
        grid=(b, SWA_KV_HEADS),
        in_specs=[pl.BlockSpec(memory_space=pltpu.SMEM),
                  pl.BlockSpec((None, s, SWA_STEP_COLS), lambda bi, g: (bi, 0, g)),
                  pl.BlockSpec((None, per_kv * Q_BLOCK, 3 * Q_BLOCK), lambda bi, g: (g, 0, 0))],
        out_specs=pl.BlockSpec((None, s, 2 * LANES), lambda bi, g: (bi, 0, g)),
        out_shape=jax.ShapeDtypeStruct((b, s, SWA_HEADS * SWA_HEAD_DIM), BF16),
        scratch_shapes=[pltpu.VMEM((s, 2 * LANES), BF16)],
        compiler_params=_params(2),
        name="swa",
    )(sink, qkv, bias)


def _diff_body(lam_ref, far_ref, x_ref, near_ref, subln_ref, o_ref, vx_ref, *, lambda_init):
    head = pl.program_id(0)
    seq = x_ref.shape[0]
    tq = DIFF_Q_ROWS
    lv = lam_ref[...]
    lam = (jnp.exp(jnp.sum(lv[0:1] * lv[1:2], axis=-1, keepdims=True))
           - jnp.exp(jnp.sum(lv[2:3] * lv[3:4], axis=-1, keepdims=True)) + lambda_init)
    c_left, c_right = far_ref[2 * head], far_ref[2 * head + 1]
    low = _low_half((tq, LANES))
    k = x_ref[:, LANES:2 * LANES]
    _with_ones(vx_ref, x_ref[:, 2 * LANES:])
    vx = vx_ref[...]
    scores = {}

    def score_unit(qt):
        q = x_ref[qt * tq:(qt + 1) * tq, :LANES]
        lhs = jnp.concatenate([_half(q, low, 0), _half(q, low, 1)], axis=0)
        scores[qt] = _dot_nt(lhs, k)

    def value_unit(qt):
        s = scores.pop(qt)
        a = max(0, qt - 1) * tq
        b = min(seq // tq, qt + 2) * tq
        t0 = a - (qt - 1) * tq
        bias = near_ref[:, t0:t0 + (b - a)]
        near = s[:, a:b] + jnp.concatenate([bias, bias], axis=0)
        m = jnp.max(near, axis=-1, keepdims=True)
        if a > 0:
            m = jnp.maximum(m, jnp.max(s[:, :a], axis=-1, keepdims=True) + c_left)
        if b < seq:
            m = jnp.maximum(m, jnp.max(s[:, b:], axis=-1, keepdims=True) + c_right)
        parts = []
        if a > 0:
            parts.append(jnp.exp2(s[:, :a] - (m - c_left)).astype(BF16))
        parts.append(jnp.exp2(near - m).astype(BF16))
        if b < seq:
            parts.append(jnp.exp2(s[:, b:] - (m - c_right)).astype(BF16))
        ol = _dot(jnp.concatenate(parts, axis=1), vx)
        on = ol[:, :LANES] / ol[:, LANES:]
        o = on[:tq] - lam * on[tq:]
        o_ref[qt * tq:(qt + 1) * tq, :] = (_rms(o, subln_ref[...])
                                           * (1.0 - lambda_init)).astype(BF16)

    _skewed(seq // tq, DIFF_SCORE_LEAD, score_unit, value_unit)


def _diff(layer, lambda_init, lam_vecs, far, qkv, near, subln):
    b, s, _ = qkv.shape
    head_block = lambda hd, bi: (bi, 0, hd)
    return pl.pallas_call(
        functools.partial(_diff_body, lambda_init=lambda_init),
        grid=(DIFF_HEADS, b),
        in_specs=[pl.BlockSpec((None, 4, DIFF_HEAD_DIM), lambda hd, bi: (layer, 0, 0)),
                  pl.BlockSpec(memory_space=pltpu.SMEM),
                  pl.BlockSpec((None, s, DIFF_STEP_COLS), head_block),
                  pl.BlockSpec((None, DIFF_Q_ROWS, 3 * DIFF_Q_ROWS), lambda hd, bi: (hd, 0, 0)),
                  pl.BlockSpec((None, 1, LANES), lambda hd, bi: (layer, 0, 0))],
        out_specs=pl.BlockSpec((None, s, LANES), head_block),
        out_shape=jax.ShapeDtypeStruct((b, s, DIFF_HEADS * 2 * DIFF_HEAD_DIM), BF16),
        scratch_shapes=[pltpu.VMEM((s, 2 * LANES), BF16)],
        compiler_params=_params(2),
        name="diff",
    )(lam_vecs, far, qkv, near, subln)


def _post_body(h_ref, sg_ref, oa_ref, ob_ref, oc_ref, wo_ref, gain_ref, wg_ref, wu_ref, wd_ref,
               p_ref, pgain_ref, wpg_ref, wpp_ref, fin_ref, o_ref, *, final):
    d = D_MODEL
    halves = _row_halves(h_ref.shape[0])
    merged = [(sg_ref[r, :d].astype(F32) * oa_ref[r, :].astype(F32)
               + sg_ref[r, d:2 * d].astype(F32) * ob_ref[r, :].astype(F32)
               + sg_ref[r, 2 * d:].astype(F32) * oc_ref[r, :].astype(F32)).astype(BF16)
              for r in halves]
    hs = [h_ref[r, :] + _dot(m, wo_ref[...]) for r, m in zip(halves, merged)]
    hs = _swiglu_half_step(hs, gain_ref, wg_ref, wu_ref, wd_ref)
    gates = [jax.nn.sigmoid(_dot(_rms(h, pgain_ref[...]).astype(BF16), wpg_ref[...])) for h in hs]
    for r, h, gate in zip(halves, hs, gates):
        h = h + gate * _dot(p_ref[r, :].astype(BF16), wpp_ref[...])
        o_ref[r, :] = _rms(h, fin_ref[...]) if final else h


def _post(h, layer, sg, oa, ob, oc, w_out, gain, wg, wu, wd, p, pgain, wpg, wpp, final_norm, final):
    n = h.shape[0]
    tm = POST_ROWS
    row = lambda r: (r, 0)
    lay3 = lambda r: (layer, 0, 0)
    res = lambda a: _resident((None,) + a.shape[1:], lay3)
    return pl.pallas_call(
        functools.partial(_post_body, final=final),
        grid=(n // tm,),
        in_specs=[pl.BlockSpec((tm, D_MODEL), row), pl.BlockSpec((tm, 3 * D_MODEL), row),
                  pl.BlockSpec((tm, D_MODEL), row), pl.BlockSpec((tm, D_MODEL), row),
                  pl.BlockSpec((tm, D_MODEL), row),
                  res(w_out), res(gain), res(wg), res(wu), res(wd),
                  pl.BlockSpec((None, tm, PLE_DIM), lambda r: (layer, r, 0)),
                  res(pgain), res(wpg), res(wpp), _resident((1, D_MODEL), lambda r: (0, 0))],
        out_specs=pl.BlockSpec((tm, D_MODEL), row),
        out_shape=jax.ShapeDtypeStruct(h.shape, F32),
        compiler_params=_params(1),
        name="post",
    )(h, sg, oa, ob, oc, w_out, gain, wg, wu, wd, p, pgain, wpg, wpp, final_norm)


def _bias_body(table_ref, idx_ref, o_ref, *, first_head, n_heads):
    head = first_head + pl.program_id(0)
    idx = idx_ref[...]
    acc = jnp.full(idx.shape, NEG_INF, F32)
    for bucket in range(REL_BUCKETS):
        acc = jnp.where(idx == bucket, table_ref[bucket * n_heads + head] * LOG2E, acc)
    o_ref[...] = acc


def _bias_tiles(table_flat, idx, first_head, count, n_heads):
    return pl.pallas_call(
        functools.partial(_bias_body, first_head=first_head, n_heads=n_heads),
        grid=(count,),
        in_specs=[pl.BlockSpec(memory_space=pltpu.SMEM),
                  _resident(idx.shape, lambda hd: (0, 0))],
        out_specs=pl.BlockSpec((None,) + idx.shape, lambda hd: (hd, 0, 0)),
        out_shape=jax.ShapeDtypeStruct((count,) + idx.shape, F32),
        compiler_params=_params(1),
        name="bias",
    )(table_flat, idx)


def _t5_bucket_np(rel):
    nb = REL_BUCKETS // 2
    max_exact = nb // 2
    base = np.where(rel > 0, nb, 0)
    n = np.abs(rel)
    nf = np.maximum(n, 1).astype(np.float64)
    large = max_exact + (np.log(nf / max_exact) / math.log(REL_MAX_DIST / max_exact)
                         * (nb - max_exact)).astype(np.int64)
    large = np.minimum(large, nb - 1)
    return (base + np.where(n < max_exact, n, large)).astype(np.int32)


def _bias_indices():
    rel = (np.arange(3 * Q_BLOCK)[None, :] - WINDOW) - np.arange(Q_BLOCK)[:, None]
    idx_b = np.where(np.abs(rel) <= WINDOW, _t5_bucket_np(rel), REL_BUCKETS).astype(np.int32)
    t = DIFF_Q_ROWS
    rel = (np.arange(3 * t)[None, :] - t) - np.arange(t)[:, None]
    idx_c = _t5_bucket_np(rel)
    far = _t5_bucket_np(np.array([-t, t]))
    assert (far == _t5_bucket_np(np.array([-REL_MAX_DIST, REL_MAX_DIST]))).all()
    return idx_b, idx_c, far


def _rope_tables(seq):
    half = MLA_ROPE // 2
    inv = ROPE_THETA ** (-jnp.arange(0, MLA_ROPE, 2, dtype=F32) / MLA_ROPE)
    ang = jnp.arange(seq, dtype=F32)[:, None] * inv[None, :]
    cos, sin = jnp.cos(ang), jnp.sin(ang)
    z = lambda w: jnp.zeros((seq, w), F32)
    cos_t = jnp.concatenate([z(MLA_NOPE), cos, cos, z(LANES - MLA_NOPE - MLA_ROPE)], axis=1)
    sin_lo = jnp.concatenate([z(MLA_NOPE), -sin, z(LANES - MLA_NOPE - half)], axis=1)
    sin_hi = jnp.concatenate([z(MLA_NOPE + half), sin, z(LANES - MLA_NOPE - MLA_ROPE)], axis=1)
    return cos_t, sin_lo, sin_hi


def _dup_heads(w, heads, dim):
    w4 = w.reshape(w.shape[:2] + (heads, dim))
    return jnp.concatenate([w4, w4], axis=-1).reshape(w.shape[:2] + (heads * 2 * dim,))


def kernel(x, p, ffn1_norm, ffn1_w_gate, ffn1_w_up, ffn1_w_down, mix_norm, w_in, mla_q_norm, mla_w_uq, mla_kv_norm, mla_w_ukv, swa_sink, diff_lambda_q1, diff_lambda_k1, diff_lambda_q2, diff_lambda_k2, diff_subln, rel_table, w_out, ffn2_norm, ffn2_w_gate, ffn2_w_up, ffn2_w_down, ple_norm, ple_w_gate, ple_w_proj, final_norm):
    b, s, d = x.shape
    depth = w_in.shape[0]
    n = b * s
    bf = lambda a: a.astype(BF16)
    vec = lambda a: a.astype(F32)[:, None, :]

    o = np.cumsum((0, MLA_Q_LORA, MLA_KV_LORA, MLA_ROPE, 1024, 256, 256, 1024, 1024, 1024, 3072))
    w_in_bf = bf(w_in)
    col = lambda i: w_in_bf[:, :, o[i]:o[i + 1]]
    zeros = lambda w: jnp.zeros((depth, d, w), BF16)
    wlat = bf(jnp.concatenate([col(0), col(1), zeros(MLA_NOPE), col(2),
                               zeros(LANES - MLA_NOPE - MLA_ROPE)], axis=-1))
    by = lambda w, groups: w.reshape(depth, d, groups, w.shape[-1] // groups)
    wb = bf(jnp.concatenate([by(col(3), SWA_KV_HEADS),
                             by(_dup_heads(col(4), SWA_KV_HEADS, SWA_HEAD_DIM), SWA_KV_HEADS),
                             by(_dup_heads(col(5), SWA_KV_HEADS, SWA_HEAD_DIM), SWA_KV_HEADS)],
                            axis=-1).reshape(depth, d, SWA_KV_HEADS * SWA_STEP_COLS))
    wc = bf(jnp.concatenate([by(col(6), DIFF_HEADS), by(col(7), DIFF_HEADS), by(col(8), DIFF_HEADS)],
                            axis=-1).reshape(depth, d, DIFF_HEADS * DIFF_STEP_COLS))
    q_scale = lambda dim, q_cols, step_cols, steps: jnp.tile(jnp.concatenate(
        [jnp.full((q_cols,), dim ** -0.5 * LOG2E, F32), jnp.ones((step_cols - q_cols,), F32)]),
        steps)[None, :]
    sb = q_scale(SWA_HEAD_DIM, 2 * LANES, SWA_STEP_COLS, SWA_KV_HEADS)
    sc = q_scale(DIFF_HEAD_DIM, LANES, DIFF_STEP_COLS, DIFF_HEADS)
    wgate = bf(col(9))
    uq = mla_w_uq.reshape(depth, MLA_Q_LORA, MLA_HEADS, MLA_NOPE + MLA_ROPE)
    wuq = bf(jnp.pad(uq, ((0, 0),) * 3 + ((0, LANES - MLA_NOPE - MLA_ROPE),))
             .reshape(depth, MLA_Q_LORA, MLA_HEADS * LANES))
    ukv = mla_w_ukv.reshape(depth, MLA_KV_LORA, MLA_HEADS, MLA_NOPE + MLA_V)
    wuk = bf(jnp.pad(ukv[..., :MLA_NOPE], ((0, 0),) * 3 + ((0, LANES - MLA_NOPE),))
             .reshape(depth, MLA_KV_LORA, MLA_HEADS * LANES))
    wuv = bf(ukv[..., MLA_NOPE:].reshape(depth, MLA_KV_LORA, MLA_HEADS * MLA_V))

    cos_t, sin_lo, sin_hi = _rope_tables(s)
    idx_b, idx_c, far_buckets = _bias_indices()
    n_bias_heads = SWA_HEADS + DIFF_HEADS
    table_flat = rel_table.astype(F32).reshape(REL_BUCKETS * n_bias_heads)
    per_kv = SWA_HEADS // SWA_KV_HEADS
    bias_b = _bias_tiles(table_flat, jnp.asarray(idx_b), 0, SWA_HEADS, n_bias_heads).reshape(
        SWA_KV_HEADS, per_kv * Q_BLOCK, 3 * Q_BLOCK)
    near_c = _bias_tiles(table_flat, jnp.asarray(idx_c), SWA_HEADS, DIFF_HEADS, n_bias_heads)
    far_c = (jnp.stack([rel_table[int(far_buckets[0]), SWA_HEADS:],
                        rel_table[int(far_buckets[1]), SWA_HEADS:]], axis=1).astype(F32)
             * LOG2E).reshape(2 * DIFF_HEADS)
    sink = swa_sink.astype(F32) * LOG2E
    lam_vecs = jnp.stack([diff_lambda_q1, diff_lambda_k1, diff_lambda_q2, diff_lambda_k2],
                         axis=1).astype(F32)
    f1 = (vec(ffn1_norm), bf(ffn1_w_gate), bf(ffn1_w_up), bf(ffn1_w_down))
    f2 = (vec(ffn2_norm), bf(ffn2_w_gate), bf(ffn2_w_up), bf(ffn2_w_down))
    mixn, qn, kvn, subln, plen = (vec(mix_norm), vec(mla_q_norm), vec(mla_kv_norm),
                                  vec(diff_subln), vec(ple_norm))
    wo, wpg, wpp = bf(w_out), bf(ple_w_gate), bf(ple_w_proj)
    fin = final_norm.astype(F32)[None, :]
    p2 = p.reshape(depth, n, PLE_DIM)

    h = x.reshape(n, d).astype(F32)
    for i in range(depth):
        h = _ffn(h, i, *f1)
        qkv_a, qkv_b, qkv_c, sg = _proj(h, i, s, mixn, wlat, wb, wc, wgate, qn, kvn, wuq, wuk, wuv,
                                        sb, sc, cos_t, sin_lo, sin_hi)
        r3 = lambda a: a.reshape(b, s, a.shape[-1])
        oa = _mla(r3(qkv_a))
        ob = _swa(sink[i], r3(qkv_b), bias_b)
        lambda_init = 0.8 - 0.6 * math.exp(-0.3 * i)
        oc = _diff(i, lambda_init, lam_vecs, far_c, r3(qkv_c), near_c, subln)
        h = _post(h, i, sg, oa.reshape(n, d), ob.reshape(n, d), oc.reshape(n, d), wo, *f2,
                  p2, plen, wpg, wpp, fin, final=(i == depth - 1))
    return h.reshape(b, s, d).astype(x.dtype)
```

```python
import functools
import math

import jax
import jax.numpy as jnp
import numpy as np
from jax import lax
from jax.experimental import pallas as pl
from jax.experimental.pallas import tpu as pltpu

F32 = jnp.float32
BF16 = jnp.bfloat16

D_MODEL = 1024
D_FF = 2816
PLE_DIM = 256
NORM_EPS = 1e-6
NEG_INF = -1e30
ROPE_THETA = 10000.0
MLA_HEADS = 16
MLA_NOPE = 64
MLA_ROPE = 32
MLA_V = 64
MLA_Q_LORA = 384
MLA_KV_LORA = 256
SWA_HEADS = 16
SWA_KV_HEADS = 4
SWA_HEAD_DIM = 64
WINDOW = 128
Q_BLOCK = 128
DIFF_HEADS = 8
DIFF_HEAD_DIM = 64
REL_BUCKETS = 32
REL_MAX_DIST = 128
LOG2E = math.log2(math.e)

LANES = 128
VMEM_LIMIT = 56 * 1024 * 1024

FFN_ROWS = 1024
PROJ_ROWS = 512
POST_ROWS = 512
MLA_Q_ROWS = 128
DIFF_Q_ROWS = 128
MLA_SCORE_LEAD = 3
DIFF_SCORE_LEAD = 2
MLA_STEP_COLS = 5 * LANES
SWA_STEP_COLS = 4 * LANES
DIFF_STEP_COLS = 3 * LANES


def _params(n_axes):
    return pltpu.CompilerParams(dimension_semantics=("arbitrary",) * n_axes,
                                vmem_limit_bytes=VMEM_LIMIT)


def _resident(shape, index_map):
    return pl.BlockSpec(shape, index_map, pipeline_mode=pl.Buffered(1))


def _rms(x, gain):
    return x * lax.rsqrt(jnp.mean(x * x, axis=-1, keepdims=True) + NORM_EPS) * gain


def _dot(a, b):
    return jnp.dot(a, b, preferred_element_type=F32)


def _dot_nt(a, b):
    return lax.dot_general(a, b, (((1,), (1,)), ((), ())), preferred_element_type=F32)


def _row_halves(n_rows):
    return [slice(i * (n_rows // 2), (i + 1) * (n_rows // 2)) for i in range(2)]


def _swiglu_half_step(hs, gain_ref, wg_ref, wu_ref, wd_ref):
    ns = [_rms(h, gain_ref[...]).astype(BF16) for h in hs]
    gus = [(_dot(n, wg_ref[...]), _dot(n, wu_ref[...])) for n in ns]
    acts = [(g * jax.nn.sigmoid(g) * u).astype(BF16) for g, u in gus]
    return [h + 0.5 * _dot(a, wd_ref[...]) for h, a in zip(hs, acts)]


def _ffn_body(h_ref, gain_ref, wg_ref, wu_ref, wd_ref, o_ref):
    halves = _row_halves(h_ref.shape[0])
    outs = _swiglu_half_step([h_ref[r, :] for r in halves], gain_ref, wg_ref, wu_ref, wd_ref)
    for r, o in zip(halves, outs):
        o_ref[r, :] = o


def _ffn(h, layer, gain, wg, wu, wd):
    n = h.shape[0]
    tm = FFN_ROWS
    row = lambda r: (r, 0)
    lay3 = lambda r: (layer, 0, 0)
    return pl.pallas_call(
        _ffn_body,
        grid=(n // tm,),
        in_specs=[pl.BlockSpec((tm, D_MODEL), row),
                  _resident((None, 1, D_MODEL), lay3),
                  _resident((None, D_MODEL, D_FF), lay3),
                  _resident((None, D_MODEL, D_FF), lay3),
                  _resident((None, D_FF, D_MODEL), lay3)],
        out_specs=pl.BlockSpec((tm, D_MODEL), row),
        out_shape=jax.ShapeDtypeStruct(h.shape, F32),
        compiler_params=_params(1),
        name="ffn",
    )(h, gain, wg, wu, wd)


def _rope(x, cos_t, sin_lo, sin_hi):
    return (x * cos_t + pltpu.roll(x, LANES - MLA_ROPE // 2, 1) * sin_lo
            + pltpu.roll(x, MLA_ROPE // 2, 1) * sin_hi)


def _proj_body(h_ref, gain_ref, wlat_ref, wb_ref, wc_ref, qn_ref, kvn_ref,
               wuq_ref, wuk_ref, wuv_ref, sb_ref, sc_ref, cos_ref, slo_ref, shi_ref,
               a_ref, b_ref, c_ref):
    halves = _row_halves(h_ref.shape[0])
    us = [_rms(h_ref[r, :], gain_ref[...]).astype(BF16) for r in halves]
    lats = [_dot(u, wlat_ref[...]) for u in us]
    for r, u in zip(halves, us):
        b_ref[r, :] = (_dot(u, wb_ref[...]) * sb_ref[...]).astype(BF16)
    cqs = [_rms(lat[:, :MLA_Q_LORA], qn_ref[...]).astype(BF16) for lat in lats]
    ckvs = [_rms(lat[:, MLA_Q_LORA:MLA_Q_LORA + MLA_KV_LORA], kvn_ref[...]).astype(BF16)
            for lat in lats]
    ups = [(_dot(cq, wuq_ref[...]),
            _dot(ckv, wuk_ref[...]),
            _dot(ckv, wuv_ref[...]))
           for cq, ckv in zip(cqs, ckvs)]
    for r, u in zip(halves, us):
        c_ref[r, :] = (_dot(u, wc_ref[...]) * sc_ref[...]).astype(BF16)

    scale = (MLA_NOPE + MLA_ROPE) ** -0.5 * LOG2E
    for r, lat, (q, kn, v) in zip(halves, lats, ups):
        cos_t, sin_lo, sin_hi = cos_ref[r, :], slo_ref[r, :], shi_ref[r, :]
        lane = lax.broadcasted_iota(jnp.int32, cos_t.shape, 1)
        q_cos = (cos_t + jnp.where(lane < MLA_NOPE, 1.0, 0.0)) * scale
        q_lo, q_hi = sin_lo * scale, sin_hi * scale
        kr = lat[:, MLA_Q_LORA + MLA_KV_LORA:]
        kr_rot = _rope(kr, cos_t, sin_lo, sin_hi)
        for hd in range(MLA_HEADS):
            sl = slice(hd * LANES, (hd + 1) * LANES)
            base = (hd // 2) * MLA_STEP_COLS + (hd % 2) * LANES
            a_ref[r, base:base + LANES] = _rope(q[:, sl], q_cos, q_lo, q_hi).astype(BF16)
            a_ref[r, base + 2 * LANES:base + 3 * LANES] = (kn[:, sl] + kr_rot).astype(BF16)
        for pair in range(MLA_HEADS // 2):
            base = pair * MLA_STEP_COLS + 4 * LANES
            a_ref[r, base:base + LANES] = v[:, pair * LANES:(pair + 1) * LANES].astype(BF16)


def _proj(h, layer, seq, gain, wlat, wb, wc, qn, kvn, wuq, wuk, wuv, sb, sc, cos_t, sin_lo, sin_hi):
    n = h.shape[0]
    tm = PROJ_ROWS
    row = lambda r: (r, 0)
    lay3 = lambda r: (layer, 0, 0)
    pos = lambda r: (r % (seq // tm), 0)
    widths = (MLA_HEADS // 2 * MLA_STEP_COLS, SWA_KV_HEADS * SWA_STEP_COLS,
              DIFF_HEADS * DIFF_STEP_COLS)
    res = lambda a: _resident((None,) + a.shape[1:], lay3)
    const = lambda a: _resident(a.shape, lambda r: (0, 0))
    return pl.pallas_call(
        _proj_body,
        grid=(n // tm,),
        in_specs=[pl.BlockSpec((tm, D_MODEL), row), res(gain), res(wlat), res(wb), res(wc),
                  res(qn), res(kvn), res(wuq), res(wuk), res(wuv), const(sb), const(sc),
                  pl.BlockSpec((tm, LANES), pos), pl.BlockSpec((tm, LANES), pos),
                  pl.BlockSpec((tm, LANES), pos)],
        out_specs=[pl.BlockSpec((tm, w), row) for w in widths],
        out_shape=[jax.ShapeDtypeStruct((n, w), BF16) for w in widths],
        compiler_params=_params(1),
        name="proj",
    )(h, gain, wlat, wb, wc, qn, kvn, wuq, wuk, wuv, sb, sc, cos_t, sin_lo, sin_hi)


def _low_half(shape):
    return lax.broadcasted_iota(jnp.int32, shape, len(shape) - 1) < LANES // 2


def _half(x, low, t):
    return jnp.where(low if t == 0 else jnp.logical_not(low), x, jnp.zeros_like(x))


def _with_ones(vx_ref, v):
    vx_ref[:, :LANES] = v
    vx_ref[:, LANES:] = jnp.ones((v.shape[0], LANES), BF16)


def _skewed(n_units, lead, score_unit, value_unit):
    for u in range(n_units + lead):
        if u < n_units:
            score_unit(u)
        if u >= lead:
            value_unit(u - lead)


def _mla_body(x_ref, o_ref, vx_ref):
    _with_ones(vx_ref, x_ref[:, 4 * LANES:])
    vx = vx_ref[...]
    tq = MLA_Q_ROWS
    low = _low_half((tq, LANES))
    scores = {}

    def score_unit(qi):
        rows = slice(qi * tq, (qi + 1) * tq)
        scores[qi] = [_dot_nt(x_ref[rows, j * LANES:(j + 1) * LANES],
                              x_ref[:, (2 + j) * LANES:(3 + j) * LANES])
                      for j in range(2)]

    def value_unit(qi):
        p = jnp.concatenate([jnp.exp2(s - jnp.max(s, axis=-1, keepdims=True)).astype(BF16)
                             for s in scores.pop(qi)], axis=0)
        ol = _dot(p, vx)
        o = ol[:, :LANES] / ol[:, LANES:]
        o_ref[qi * tq:(qi + 1) * tq, :] = jnp.where(low, o[:tq], o[tq:]).astype(BF16)

    _skewed(x_ref.shape[0] // tq, MLA_SCORE_LEAD, score_unit, value_unit)


def _mla(qkv):
    b, s, _ = qkv.shape
    pair = lambda bi, hp: (bi, 0, hp)
    return pl.pallas_call(
        _mla_body,
        grid=(b, MLA_HEADS // 2),
        in_specs=[pl.BlockSpec((None, s, MLA_STEP_COLS), pair)],
        out_specs=pl.BlockSpec((None, s, LANES), pair),
        out_shape=jax.ShapeDtypeStruct((b, s, MLA_HEADS * MLA_V), BF16),
        scratch_shapes=[pltpu.VMEM((s, 2 * LANES), BF16)],
        compiler_params=_params(2),
        name="mla",
    )(qkv)


def _swa_body(sink_ref, x_ref, bias_ref, o_ref, vx_ref):
    group = pl.program_id(1)
    per_kv = SWA_HEADS // SWA_KV_HEADS
    qb = Q_BLOCK
    n_blocks = x_ref.shape[0] // qb
    _with_ones(vx_ref, x_ref[:, 3 * LANES:])
    low = _low_half((qb, LANES))
    sink = jnp.concatenate([jnp.full((qb, LANES), sink_ref[per_kv * group + t], F32)
                            for t in range(per_kv)], axis=0)
    for j in range(n_blocks):
        rows = slice(j * qb, (j + 1) * qb)
        qa, qc = x_ref[rows, :LANES], x_ref[rows, LANES:2 * LANES]
        lhs = jnp.concatenate([_half(qa, low, 0), _half(qa, low, 1),
                               _half(qc, low, 0), _half(qc, low, 1)], axis=0)
        lo = max(j - 1, 0) * qb
        hi = min(j + 2, n_blocks) * qb
        c0 = lo - (j - 1) * qb
        s = _dot_nt(lhs, x_ref[lo:hi, 2 * LANES:3 * LANES]) + bias_ref[:, c0:c0 + (hi - lo)]
        m = jnp.maximum(jnp.max(s, axis=-1, keepdims=True), sink)
        p = jnp.exp2(s - jnp.concatenate([m] * ((hi - lo) // LANES), axis=1)).astype(BF16)
        ol = _dot(p, vx_ref[lo:hi, :])
        o = ol[:, :LANES] / (ol[:, LANES:] + jnp.exp2(sink - m))
        o_ref[rows, :LANES] = jnp.where(low, o[:qb], o[qb:2 * qb]).astype(BF16)
        o_ref[rows, LANES:] = jnp.where(low, o[2 * qb:3 * qb], o[3 * qb:]).astype(BF16)


def _swa(sink, qkv, bias):
    b, s, _ = qkv.shape
    per_kv = SWA_HEADS // SWA_KV_HEADS
    return pl.pallas_call(
        _swa_body,
        grid=(b, SWA_KV_HEADS),
        in_specs=[pl.BlockSpec(memory_space=pltpu.SMEM),
                  pl.BlockSpec((None, s, SWA_STEP_COLS), lambda bi, g: (bi, 0, g)),
                  pl.BlockSpec((None, per_kv * Q_BLOCK, 3 * Q_BLOCK), lambda bi, g: (g, 0, 0))],
        out_specs=pl.BlockSpec((None, s, 2 * LANES), lambda bi, g: (bi, 0, g)),
        out_shape=jax.ShapeDtypeStruct((b, s, SWA_HEADS * SWA_HEAD_DIM), BF16),
        scratch_shapes=[pltpu.VMEM((s, 2 * LANES), BF16)],
        compiler_params=_params(2),
        name="swa",
    )(sink, qkv, bias)


def _diff_body(lam_ref, far_ref, x_ref, near_ref, subln_ref, o_ref, vx_ref, *, lambda_init):
    head = pl.program_id(0)
    seq = x_ref.shape[0]
    tq = DIFF_Q_ROWS
    lv = lam_ref[...]
    lam = (jnp.exp(jnp.sum(lv[0:1] * lv[1:2], axis=-1, keepdims=True))
           - jnp.exp(jnp.sum(lv[2:3] * lv[3:4], axis=-1, keepdims=True)) + lambda_init)
    c_left, c_right = far_ref[2 * head], far_ref[2 * head + 1]
    low = _low_half((tq, LANES))
    k = x_ref[:, LANES:2 * LANES]
    _with_ones(vx_ref, x_ref[:, 2 * LANES:])
    vx = vx_ref[...]
    scores = {}

    def score_unit(qt):
        q = x_ref[qt * tq:(qt + 1) * tq, :LANES]
        lhs = jnp.concatenate([_half(q, low, 0), _half(q, low, 1)], axis=0)
        scores[qt] = _dot_nt(lhs, k)

    def value_unit(qt):
        s = scores.pop(qt)
        a = max(0, qt - 1) * tq
        b = min(seq // tq, qt + 2) * tq
        t0 = a - (qt - 1) * tq
        bias = near_ref[:, t0:t0 + (b - a)]
        near = s[:, a:b] + jnp.concatenate([bias, bias], axis=0)
        m = jnp.max(near, axis=-1, keepdims=True)
        if a > 0:
            m = jnp.maximum(m, jnp.max(s[:, :a], axis=-1, keepdims=True) + c_left)
        if b < seq:
            m = jnp.maximum(m, jnp.max(s[:, b:], axis=-1, keepdims=True) + c_right)
        parts = []
        if a > 0:
            parts.append(jnp.exp2(s[:, :a] - (m - c_left)).astype(BF16))
        parts.append(jnp.exp2(near - m).astype(BF16))
        if b < seq:
            parts.append(jnp.exp2(s[:, b:] - (m - c_right)).astype(BF16))
        ol = _dot(jnp.concatenate(parts, axis=1), vx)
        on = ol[:, :LANES] / ol[:, LANES:]
        o = on[:tq] - lam * on[tq:]
        o_ref[qt * tq:(qt + 1) * tq, :] = (_rms(o, subln_ref[...])
                                           * (1.0 - lambda_init)).astype(BF16)

    _skewed(seq // tq, DIFF_SCORE_LEAD, score_unit, value_unit)


def _diff(layer, lambda_init, lam_vecs, far, qkv, near, subln):
    b, s, _ = qkv.shape
    head_block = lambda hd, bi: (bi, 0, hd)
    return pl.pallas_call(
        functools.partial(_diff_body, lambda_init=lambda_init),
        grid=(DIFF_HEADS, b),
        in_specs=[pl.BlockSpec((None, 4, DIFF_HEAD_DIM), lambda hd, bi: (layer, 0, 0)),
                  pl.BlockSpec(memory_space=pltpu.SMEM),
                  pl.BlockSpec((None, s, DIFF_STEP_COLS), head_block),
                  pl.BlockSpec((None, DIFF_Q_ROWS, 3 * DIFF_Q_ROWS), lambda hd, bi: (hd, 0, 0)),
                  pl.BlockSpec((None, 1, LANES), lambda hd, bi: (layer, 0, 0))],
        out_specs=pl.BlockSpec((None, s, LANES), head_block),
        out_shape=jax.ShapeDtypeStruct((b, s, DIFF_HEADS * 2 * DIFF_HEAD_DIM), BF16),
        scratch_shapes=[pltpu.VMEM((s, 2 * LANES), BF16)],
        compiler_params=_params(2),
        name="diff",
    )(lam_vecs, far, qkv, near, subln)


def _post_body(h_ref, oa_ref, ob_ref, oc_ref, mgain_ref, wgate_ref, wo_ref, gain_ref, wg_ref, wu_ref,
               wd_ref, p_ref, pgain_ref, wpg_ref, wpp_ref, fin_ref, o_ref, *, final):
    d = D_MODEL
    halves = _row_halves(h_ref.shape[0])
    us = [_rms(h_ref[r, :], mgain_ref[...]).astype(BF16) for r in halves]
    sgs = [jax.nn.sigmoid(_dot(u, wgate_ref[...])) for u in us]
    merged = [(sg[:, :d] * oa_ref[r, :].astype(F32) + sg[:, d:2 * d] * ob_ref[r, :].astype(F32)
               + sg[:, 2 * d:] * oc_ref[r, :].astype(F32)).astype(BF16)
              for r, sg in zip(halves, sgs)]
    hs = [h_ref[r, :] + _dot(m, wo_ref[...]) for r, m in zip(halves, merged)]
    hs = _swiglu_half_step(hs, gain_ref, wg_ref, wu_ref, wd_ref)
    gates = [jax.nn.sigmoid(_dot(_rms(h, pgain_ref[...]).astype(BF16), wpg_ref[...])) for h in hs]
    for r, h, gate in zip(halves, hs, gates):
        h = h + gate * _dot(p_ref[r, :].astype(BF16), wpp_ref[...])
        o_ref[r, :] = _rms(h, fin_ref[...]) if final else h


def _post(h, layer, oa, ob, oc, mgain, wgate, w_out, gain, wg, wu, wd, p, pgain, wpg, wpp,
          final_norm, final):
    n = h.shape[0]
    tm = POST_ROWS
    row = lambda r: (r, 0)
    lay3 = lambda r: (layer, 0, 0)
    res = lambda a: _resident((None,) + a.shape[1:], lay3)
    return pl.pallas_call(
        functools.partial(_post_body, final=final),
        grid=(n // tm,),
        in_specs=[pl.BlockSpec((tm, D_MODEL), row), pl.BlockSpec((tm, D_MODEL), row),
                  pl.BlockSpec((tm, D_MODEL), row), pl.BlockSpec((tm, D_MODEL), row),
                  res(mgain), res(wgate), res(w_out), res(gain), res(wg), res(wu), res(wd),
                  pl.BlockSpec((None, tm, PLE_DIM), lambda r: (layer, r, 0)),
                  res(pgain), res(wpg), res(wpp), _resident((1, D_MODEL), lambda r: (0, 0))],
        out_specs=pl.BlockSpec((tm, D_MODEL), row),
        out_shape=jax.ShapeDtypeStruct(h.shape, F32),
        compiler_params=_params(1),
        name="post",
    )(h, oa, ob, oc, mgain, wgate, w_out, gain, wg, wu, wd, p, pgain, wpg, wpp, final_norm)


def _bias_body(table_ref, idx_ref, o_ref, *, first_head, n_heads):
    head = first_head + pl.program_id(0)
    idx = idx_ref[...]
    acc = jnp.full(idx.shape, NEG_INF, F32)
    for bucket in range(REL_BUCKETS):
        acc = jnp.where(idx == bucket, table_ref[bucket * n_heads + head] * LOG2E, acc)
    o_ref[...] = acc


def _bias_tiles(table_flat, idx, first_head, count, n_heads):
    return pl.pallas_call(
        functools.partial(_bias_body, first_head=first_head, n_heads=n_heads),
        grid=(count,),
        in_specs=[pl.BlockSpec(memory_space=pltpu.SMEM),
                  _resident(idx.shape, lambda hd: (0, 0))],
        out_specs=pl.BlockSpec((None,) + idx.shape, lambda hd: (hd, 0, 0)),
        out_shape=jax.ShapeDtypeStruct((count,) + idx.shape, F32),
        compiler_params=_params(1),
        name="bias",
    )(table_flat, idx)


def _t5_bucket_np(rel):
    nb = REL_BUCKETS // 2
    max_exact = nb // 2
    base = np.where(rel > 0, nb, 0)
    n = np.abs(rel)
    nf = np.maximum(n, 1).astype(np.float64)
    large = max_exact + (np.log(nf / max_exact) / math.log(REL_MAX_DIST / max_exact)
                         * (nb - max_exact)).astype(np.int64)
    large = np.minimum(large, nb - 1)
    return (base + np.where(n < max_exact, n, large)).astype(np.int32)


def _bias_indices():
    rel = (np.arange(3 * Q_BLOCK)[None, :] - WINDOW) - np.arange(Q_BLOCK)[:, None]
    idx_b = np.where(np.abs(rel) <= WINDOW, _t5_bucket_np(rel), REL_BUCKETS).astype(np.int32)
    t = DIFF_Q_ROWS
    rel = (np.arange(3 * t)[None, :] - t) - np.arange(t)[:, None]
    idx_c = _t5_bucket_np(rel)
    far = _t5_bucket_np(np.array([-t, t]))
    assert (far == _t5_bucket_np(np.array([-REL_MAX_DIST, REL_MAX_DIST]))).all()
    return idx_b, idx_c, far


def _rope_tables(seq):
    half = MLA_ROPE // 2
    inv = ROPE_THETA ** (-jnp.arange(0, MLA_ROPE, 2, dtype=F32) / MLA_ROPE)
    ang = jnp.arange(seq, dtype=F32)[:, None] * inv[None, :]
    cos, sin = jnp.cos(ang), jnp.sin(ang)
    z = lambda w: jnp.zeros((seq, w), F32)
    cos_t = jnp.concatenate([z(MLA_NOPE), cos, cos, z(LANES - MLA_NOPE - MLA_ROPE)], axis=1)
    sin_lo = jnp.concatenate([z(MLA_NOPE), -sin, z(LANES - MLA_NOPE - half)], axis=1)
    sin_hi = jnp.concatenate([z(MLA_NOPE + half), sin, z(LANES - MLA_NOPE - MLA_ROPE)], axis=1)
    return cos_t, sin_lo, sin_hi


def _dup_heads(w, heads, dim):
    w4 = w.reshape(w.shape[:2] + (heads, dim))
    return jnp.concatenate([w4, w4], axis=-1).reshape(w.shape[:2] + (heads * 2 * dim,))


def kernel(x, p, ffn1_norm, ffn1_w_gate, ffn1_w_up, ffn1_w_down, mix_norm, w_in, mla_q_norm, mla_w_uq, mla_kv_norm, mla_w_ukv, swa_sink, diff_lambda_q1, diff_lambda_k1, diff_lambda_q2, diff_lambda_k2, diff_subln, rel_table, w_out, ffn2_norm, ffn2_w_gate, ffn2_w_up, ffn2_w_down, ple_norm, ple_w_gate, ple_w_proj, final_norm):
    b, s, d = x.shape
    depth = w_in.shape[0]
    n = b * s
    bf = lambda a: a.astype(BF16)
    vec = lambda a: a.astype(F32)[:, None, :]

    o = np.cumsum((0, MLA_Q_LORA, MLA_KV_LORA, MLA_ROPE, 1024, 256, 256, 1024, 1024, 1024, 3072))
    w_in_bf = bf(w_in)
    col = lambda i: w_in_bf[:, :, o[i]:o[i + 1]]
    zeros = lambda w: jnp.zeros((depth, d, w), BF16)
    wlat = bf(jnp.concatenate([col(0), col(1), zeros(MLA_NOPE), col(2),
                               zeros(LANES - MLA_NOPE - MLA_ROPE)], axis=-1))
    by = lambda w, groups: w.reshape(depth, d, groups, w.shape[-1] // groups)
    wb = bf(jnp.concatenate([by(col(3), SWA_KV_HEADS),
                             by(_dup_heads(col(4), SWA_KV_HEADS, SWA_HEAD_DIM), SWA_KV_HEADS),
                             by(_dup_heads(col(5), SWA_KV_HEADS, SWA_HEAD_DIM), SWA_KV_HEADS)],
                            axis=-1).reshape(depth, d, SWA_KV_HEADS * SWA_STEP_COLS))
    wc = bf(jnp.concatenate([by(col(6), DIFF_HEADS), by(col(7), DIFF_HEADS), by(col(8), DIFF_HEADS)],
                            axis=-1).reshape(depth, d, DIFF_HEADS * DIFF_STEP_COLS))
    q_scale = lambda dim, q_cols, step_cols, steps: jnp.tile(jnp.concatenate(
        [jnp.full((q_cols,), dim ** -0.5 * LOG2E, F32), jnp.ones((step_cols - q_cols,), F32)]),
        steps)[None, :]
    sb = q_scale(SWA_HEAD_DIM, 2 * LANES, SWA_STEP_COLS, SWA_KV_HEADS)
    sc = q_scale(DIFF_HEAD_DIM, LANES, DIFF_STEP_COLS, DIFF_HEADS)
    wgate = bf(col(9))
    uq = mla_w_uq.reshape(depth, MLA_Q_LORA, MLA_HEADS, MLA_NOPE + MLA_ROPE)
    wuq = bf(jnp.pad(uq, ((0, 0),) * 3 + ((0, LANES - MLA_NOPE - MLA_ROPE),))
             .reshape(depth, MLA_Q_LORA, MLA_HEADS * LANES))
    ukv = mla_w_ukv.reshape(depth, MLA_KV_LORA, MLA_HEADS, MLA_NOPE + MLA_V)
    wuk = bf(jnp.pad(ukv[..., :MLA_NOPE], ((0, 0),) * 3 + ((0, LANES - MLA_NOPE),))
             .reshape(depth, MLA_KV_LORA, MLA_HEADS * LANES))
    wuv = bf(ukv[..., MLA_NOPE:].reshape(depth, MLA_KV_LORA, MLA_HEADS * MLA_V))

    cos_t, sin_lo, sin_hi = _rope_tables(s)
    idx_b, idx_c, far_buckets = _bias_indices()
    n_bias_heads = SWA_HEADS + DIFF_HEADS
    table_flat = rel_table.astype(F32).reshape(REL_BUCKETS * n_bias_heads)
    per_kv = SWA_HEADS // SWA_KV_HEADS
    bias_b = _bias_tiles(table_flat, jnp.asarray(idx_b), 0, SWA_HEADS, n_bias_heads).reshape(
        SWA_KV_HEADS, per_kv * Q_BLOCK, 3 * Q_BLOCK)
    near_c = _bias_tiles(table_flat, jnp.asarray(idx_c), SWA_HEADS, DIFF_HEADS, n_bias_heads)
    far_c = (jnp.stack([rel_table[int(far_buckets[0]), SWA_HEADS:],
                        rel_table[int(far_buckets[1]), SWA_HEADS:]], axis=1).astype(F32)
             * LOG2E).reshape(2 * DIFF_HEADS)
    sink = swa_sink.astype(F32) * LOG2E
    lam_vecs = jnp.stack([diff_lambda_q1, diff_lambda_k1, diff_lambda_q2, diff_lambda_k2],
                         axis=1).astype(F32)
    f1 = (vec(ffn1_norm), bf(ffn1_w_gate), bf(ffn1_w_up), bf(ffn1_w_down))
    f2 = (vec(ffn2_norm), bf(ffn2_w_gate), bf(ffn2_w_up), bf(ffn2_w_down))
    mixn, qn, kvn, subln, plen = (vec(mix_norm), vec(mla_q_norm), vec(mla_kv_norm),
                                  vec(diff_subln), vec(ple_norm))
    wo, wpg, wpp = bf(w_out), bf(ple_w_gate), bf(ple_w_proj)
    fin = final_norm.astype(F32)[None, :]
    p2 = p.reshape(depth, n, PLE_DIM)

    h = x.reshape(n, d).astype(F32)
    for i in range(depth):
        h = _ffn(h, i, *f1)
        qkv_a, qkv_b, qkv_c = _proj(h, i, s, mixn, wlat, wb, wc, qn, kvn, wuq, wuk, wuv,
                                    sb, sc, cos_t, sin_lo, sin_hi)
        r3 = lambda a: a.reshape(b, s, a.shape[-1])
        oa = _mla(r3(qkv_a))
        ob = _swa(sink[i], r3(qkv_b), bias_b)
        lambda_init = 0.8 - 0.6 * math.exp(-0.3 * i)
        oc = _diff(i, lambda_init, lam_vecs, far_c, r3(qkv_c), near_c, subln)
        h = _post(h, i, oa.reshape(n, d), ob.reshape(n, d), oc.reshape(n, d), mixn, wgate, wo, *f2,
                  p2, plen, wpg, wpp, fin, final=(i == depth - 1))
    return h.reshape(b, s, d).astype(x.dtype)
```

```python
import functools
import math

import jax
import jax.numpy as jnp
import numpy as np
from jax import lax
from jax.experimental import pallas as pl
from jax.experimental.pallas import tpu as pltpu

F32 = jnp.float32
BF16 = jnp.bfloat16

D_MODEL = 1024
D_FF = 2816
PLE_DIM = 256
NORM_EPS = 1e-6
NEG_INF = -1e30
ROPE_THETA = 10000.0
MLA_HEADS = 16
MLA_NOPE = 64
MLA_ROPE = 32
MLA_V = 64
MLA_Q_LORA = 384
MLA_KV_LORA = 256
SWA_HEADS = 16
SWA_KV_HEADS = 4
SWA_HEAD_DIM = 64
WINDOW = 128
Q_BLOCK = 128
DIFF_HEADS = 8
DIFF_HEAD_DIM = 64
REL_BUCKETS = 32
REL_MAX_DIST = 128
LOG2E = math.log2(math.e)

LANES = 128
VMEM_LIMIT = 56 * 1024 * 1024

FFN_ROWS = 1024
PROJ_ROWS = 512
POST_ROWS = 512
MLA_Q_ROWS = 128
DIFF_Q_ROWS = 128
MLA_SCORE_LEAD = 3
DIFF_SCORE_LEAD = 2
MLA_STEP_COLS = 5 * LANES
SWA_STEP_COLS = 4 * LANES
DIFF_STEP_COLS = 3 * LANES


def _params(n_axes):
    return pltpu.CompilerParams(dimension_semantics=("arbitrary",) * n_axes,
                                vmem_limit_bytes=VMEM_LIMIT)


def _resident(shape, index_map):
    return pl.BlockSpec(shape, index_map, pipeline_mode=pl.Buffered(1))


def _rms(x, gain):
    return x * lax.rsqrt(jnp.mean(x * x, axis=-1, keepdims=True) + NORM_EPS) * gain


def _dot(a, b):
    return jnp.dot(a, b, preferred_element_type=F32)


def _dot_nt(a, b):
    return lax.dot_general(a, b, (((1,), (1,)), ((), ())), preferred_element_type=F32)


def _row_halves(n_rows):
    return [slice(i * (n_rows // 2), (i + 1) * (n_rows // 2)) for i in range(2)]


def _swiglu_half_step(hs, gain_ref, wg_ref, wu_ref, wd_ref):
    ns = [_rms(h, gain_ref[...]).astype(BF16) for h in hs]
    gus = [(_dot(n, wg_ref[...]), _dot(n, wu_ref[...])) for n in ns]
    acts = [(g * jax.nn.sigmoid(g) * u).astype(BF16) for g, u in gus]
    return [h + 0.5 * _dot(a, wd_ref[...]) for h, a in zip(hs, acts)]


def _ffn_body(h_ref, gain_ref, wg_ref, wu_ref, wd_ref, o_ref):
    halves = _row_halves(h_ref.shape[0])
    outs = _swiglu_half_step([h_ref[r, :] for r in halves], gain_ref, wg_ref, wu_ref, wd_ref)
    for r, o in zip(halves, outs):
        o_ref[r, :] = o


def _ffn(h, layer, gain, wg, wu, wd):
    n = h.shape[0]
    tm = FFN_ROWS
    row = lambda r: (r, 0)
    lay3 = lambda r: (layer, 0, 0)
    return pl.pallas_call(
        _ffn_body,
        grid=(n // tm,),
        in_specs=[pl.BlockSpec((tm, D_MODEL), row),
                  _resident((None, 1, D_MODEL), lay3),
                  _resident((None, D_MODEL, D_FF), lay3),
                  _resident((None, D_MODEL, D_FF), lay3),
                  _resident((None, D_FF, D_MODEL), lay3)],
        out_specs=pl.BlockSpec((tm, D_MODEL), row),
        out_shape=jax.ShapeDtypeStruct(h.shape, F32),
        compiler_params=_params(1),
        name="ffn",
    )(h, gain, wg, wu, wd)


def _rope(x, cos_t, sin_lo, sin_hi):
    return (x * cos_t + pltpu.roll(x, LANES - MLA_ROPE // 2, 1) * sin_lo
            + pltpu.roll(x, MLA_ROPE // 2, 1) * sin_hi)


def _proj_body(h_ref, gain_ref, wlat_ref, wb_ref, wc_ref, qn_ref, kvn_ref,
               wuq_ref, wuk_ref, wuv_ref, sb_ref, sc_ref, cos_ref, slo_ref, shi_ref,
               a_ref, b_ref, c_ref):
    halves = _row_halves(h_ref.shape[0])
    us = [_rms(h_ref[r, :], gain_ref[...]).astype(BF16) for r in halves]
    lats = [_dot(u, wlat_ref[...]) for u in us]
    for r, u in zip(halves, us):
        b_ref[r, :] = (_dot(u, wb_ref[...]) * sb_ref[...]).astype(BF16)
    cqs = [_rms(lat[:, :MLA_Q_LORA], qn_ref[...]).astype(BF16) for lat in lats]
    ckvs = [_rms(lat[:, MLA_Q_LORA:MLA_Q_LORA + MLA_KV_LORA], kvn_ref[...]).astype(BF16)
            for lat in lats]
    ups = [(_dot(cq, wuq_ref[...]),
            _dot(ckv, wuk_ref[...]),
            _dot(ckv, wuv_ref[...]))
           for cq, ckv in zip(cqs, ckvs)]
    for r, u in zip(halves, us):
        c_ref[r, :] = (_dot(u, wc_ref[...]) * sc_ref[...]).astype(BF16)

    scale = (MLA_NOPE + MLA_ROPE) ** -0.5 * LOG2E
    for r, lat, (q, kn, v) in zip(halves, lats, ups):
        cos_t, sin_lo, sin_hi = cos_ref[r, :], slo_ref[r, :], shi_ref[r, :]
        lane = lax.broadcasted_iota(jnp.int32, cos_t.shape, 1)
        q_cos = (cos_t + jnp.where(lane < MLA_NOPE, 1.0, 0.0)) * scale
        q_lo, q_hi = sin_lo * scale, sin_hi * scale
        kr = lat[:, MLA_Q_LORA + MLA_KV_LORA:]
        kr_rot = _rope(kr, cos_t, sin_lo, sin_hi)
        for hd in range(MLA_HEADS):
            sl = slice(hd * LANES, (hd + 1) * LANES)
            base = (hd // 2) * MLA_STEP_COLS + (hd % 2) * LANES
            a_ref[r, base:base + LANES] = _rope(q[:, sl], q_cos, q_lo, q_hi).astype(BF16)
            a_ref[r, base + 2 * LANES:base + 3 * LANES] = (kn[:, sl] + kr_rot).astype(BF16)
        for pair in range(MLA_HEADS // 2):
            base = pair * MLA_STEP_COLS + 4 * LANES
            a_ref[r, base:base + LANES] = v[:, pair * LANES:(pair + 1) * LANES].astype(BF16)


def _proj(h, layer, seq, gain, wlat, wb, wc, qn, kvn, wuq, wuk, wuv, sb, sc, cos_t, sin_lo, sin_hi):
    n = h.shape[0]
    tm = PROJ_ROWS
    row = lambda r: (r, 0)
    lay3 = lambda r: (layer, 0, 0)
    pos = lambda r: (r % (seq // tm), 0)
    widths = (MLA_HEADS // 2 * MLA_STEP_COLS, SWA_KV_HEADS * SWA_STEP_COLS,
              DIFF_HEADS * DIFF_STEP_COLS)
    res = lambda a: _resident((None,) + a.shape[1:], lay3)
    const = lambda a: _resident(a.shape, lambda r: (0, 0))
    return pl.pallas_call(
        _proj_body,
        grid=(n // tm,),
        in_specs=[pl.BlockSpec((tm, D_MODEL), row), res(gain), res(wlat), res(wb), res(wc),
                  res(qn), res(kvn), res(wuq), res(wuk), res(wuv), const(sb), const(sc),
                  pl.BlockSpec((tm, LANES), pos), pl.BlockSpec((tm, LANES), pos),
                  pl.BlockSpec((tm, LANES), pos)],
        out_specs=[pl.BlockSpec((tm, w), row) for w in widths],
        out_shape=[jax.ShapeDtypeStruct((n, w), BF16) for w in widths],
        compiler_params=_params(1),
        name="proj",
    )(h, gain, wlat, wb, wc, qn, kvn, wuq, wuk, wuv, sb, sc, cos_t, sin_lo, sin_hi)


def _low_half(shape):
    return lax.broadcasted_iota(jnp.int32, shape, len(shape) - 1) < LANES // 2


def _half(x, low, t):
    return jnp.where(low if t == 0 else jnp.logical_not(low), x, jnp.zeros_like(x))


def _with_ones(vx_ref, v):
    vx_ref[:, :LANES] = v
    vx_ref[:, LANES:] = jnp.ones((v.shape[0], LANES), BF16)


def _skewed(n_units, lead, score_unit, value_unit):
    for u in range(n_units + lead):
        if u < n_units:
            score_unit(u)
        if u >= lead:
            value_unit(u - lead)


def _mla_body(x_ref, o_ref, vx_ref):
    _with_ones(vx_ref, x_ref[:, 4 * LANES:])
    vx = vx_ref[...]
    tq = MLA_Q_ROWS
    low = _low_half((tq, LANES))
    scores = {}

    def score_unit(qi):
        rows = slice(qi * tq, (qi + 1) * tq)
        scores[qi] = [_dot_nt(x_ref[rows, j * LANES:(j + 1) * LANES],
                              x_ref[:, (2 + j) * LANES:(3 + j) * LANES])
                      for j in range(2)]

    def value_unit(qi):
        p = jnp.concatenate([jnp.exp2(s - jnp.max(s, axis=-1, keepdims=True)).astype(BF16)
                             for s in scores.pop(qi)], axis=0)
        ol = _dot(p, vx)
        o = ol[:, :LANES] / ol[:, LANES:]
        o_ref[qi * tq:(qi + 1) * tq, :] = jnp.where(low, o[:tq], o[tq:]).astype(BF16)

    _skewed(x_ref.shape[0] // tq, MLA_SCORE_LEAD, score_unit, value_unit)


def _mla(qkv):
    b, s, _ = qkv.shape
    pair = lambda bi, hp: (bi, 0, hp)
    return pl.pallas_call(
        _mla_body,
        grid=(b, MLA_HEADS // 2),
        in_specs=[pl.BlockSpec((None, s, MLA_STEP_COLS), pair)],
        out_specs=pl.BlockSpec((None, s, LANES), pair),
        out_shape=jax.ShapeDtypeStruct((b, s, MLA_HEADS * MLA_V), BF16),
        scratch_shapes=[pltpu.VMEM((s, 2 * LANES), BF16)],
        compiler_params=_params(2),
        name="mla",
    )(qkv)


def _swa_body(sink_ref, x_ref, bias_ref, o_ref, vx_ref):
    group = pl.program_id(1)
    per_kv = SWA_HEADS // SWA_KV_HEADS
    qb = Q_BLOCK
    n_blocks = x_ref.shape[0] // qb
    _with_ones(vx_ref, x_ref[:, 3 * LANES:])
    low = _low_half((qb, LANES))
    sink = jnp.concatenate([jnp.full((qb, LANES), sink_ref[per_kv * group + t], F32)
                            for t in range(per_kv)], axis=0)
    for j in range(n_blocks):
        rows = slice(j * qb, (j + 1) * qb)
        qa, qc = x_ref[rows, :LANES], x_ref[rows, LANES:2 * LANES]
        lhs = jnp.concatenate([_half(qa, low, 0), _half(qa, low, 1),
                               _half(qc, low, 0), _half(qc, low, 1)], axis=0)
        lo = max(j - 1, 0) * qb
        hi = min(j + 2, n_blocks) * qb
        c0 = lo - (j - 1) * qb
        s = _dot_nt(lhs, x_ref[lo:hi, 2 * LANES:3 * LANES]) + bias_ref[:, c0:c0 + (hi - lo)]
        m = jnp.maximum(jnp.max(s, axis=-1, keepdims=True), sink)
        p = jnp.exp2(s - jnp.concatenate([m] * ((hi - lo) // LANES), axis=1)).astype(BF16)
        ol = _dot(p, vx_ref[lo:hi, :])
        o = ol[:, :LANES] / (ol[:, LANES:] + jnp.exp2(sink - m))
        o_ref[rows, :LANES] = jnp.where(low, o[:qb], o[qb:2 * qb]).astype(BF16)
        o_ref[rows, LANES:] = jnp.where(low, o[2 * qb:3 * qb], o[3 * qb:]).astype(BF16)


def _swa(sink, qkv, bias):
    b, s, _ = qkv.shape
    per_kv = SWA_HEADS // SWA_KV_HEADS
    return pl.pallas_call(
        _swa_body,
        grid=(b, SWA_KV_HEADS),
        in_specs=[pl.BlockSpec(memory_space=pltpu.SMEM),
                  pl.BlockSpec((None, s, SWA_STEP_COLS), lambda bi, g: (bi, 0, g)),
                  pl.BlockSpec((None, per_kv * Q_BLOCK, 3 * Q_BLOCK), lambda bi, g: (g, 0, 0))],
        out_specs=pl.BlockSpec((None, s, 2 * LANES), lambda bi, g: (bi, 0, g)),
        out_shape=jax.ShapeDtypeStruct((b, s, SWA_HEADS * SWA_HEAD_DIM), BF16),
        scratch_shapes=[pltpu.VMEM((s, 2 * LANES), BF16)],
        compiler_params=_params(2),
        name="swa",
    )(sink, qkv, bias)


def _diff_body(lam_ref, far_ref, x_ref, near_ref, subln_ref, o_ref, vx_ref, *, lambda_init):
    head = pl.program_id(0)
    seq = x_ref.shape[0]
    tq = DIFF_Q_ROWS
    lv = lam_ref[...]
    lam = (jnp.exp(jnp.sum(lv[0:1] * lv[1:2], axis=-1, keepdims=True))
           - jnp.exp(jnp.sum(lv[2:3] * lv[3:4], axis=-1, keepdims=True)) + lambda_init)
    c_left, c_right = far_ref[2 * head], far_ref[2 * head + 1]
    low = _low_half((tq, LANES))
    k = x_ref[:, LANES:2 * LANES]
    _with_ones(vx_ref, x_ref[:, 2 * LANES:])
    vx = vx_ref[...]
    scores = {}

    def score_unit(qt):
        q = x_ref[qt * tq:(qt + 1) * tq, :LANES]
        lhs = jnp.concatenate([_half(q, low, 0), _half(q, low, 1)], axis=0)
        scores[qt] = _dot_nt(lhs, k)

    def value_unit(qt):
        s = scores.pop(qt)
        a = max(0, qt - 1) * tq
        b = min(seq // tq, qt + 2) * tq
        t0 = a - (qt - 1) * tq
        bias = near_ref[:, t0:t0 + (b - a)]
        near = s[:, a:b] + jnp.concatenate([bias, bias], axis=0)
        m = jnp.max(near, axis=-1, keepdims=True)
        if a > 0:
            m = jnp.maximum(m, jnp.max(s[:, :a], axis=-1, keepdims=True) + c_left)
        if b < seq:
            m = jnp.maximum(m, jnp.max(s[:, b:], axis=-1, keepdims=True) + c_right)
        parts = []
        if a > 0:
            parts.append(jnp.exp2(s[:, :a] - (m - c_left)).astype(BF16))
        parts.append(jnp.exp2(near - m).astype(BF16))
        if b < seq:
            parts.append(jnp.exp2(s[:, b:] - (m - c_right)).astype(BF16))
        ol = _dot(jnp.concatenate(parts, axis=1), vx)
        on = ol[:, :LANES] / ol[:, LANES:]
        o = on[:tq] - lam * on[tq:]
        o_ref[qt * tq:(qt + 1) * tq, :] = (_rms(o, subln_ref[...])
                                           * (1.0 - lambda_init)).astype(BF16)

    _skewed(seq // tq, DIFF_SCORE_LEAD, score_unit, value_unit)


def _diff(layer, lambda_init, lam_vecs, far, qkv, near, subln):
    b, s, _ = qkv.shape
    head_block = lambda hd, bi: (bi, 0, hd)
    return pl.pallas_call(
        functools.partial(_diff_body, lambda_init=lambda_init),
        grid=(DIFF_HEADS, b),
        in_specs=[pl.BlockSpec((None, 4, DIFF_HEAD_DIM), lambda hd, bi: (layer, 0, 0)),
                  pl.BlockSpec(memory_space=pltpu.SMEM),
                  pl.BlockSpec((None, s, DIFF_STEP_COLS), head_block),
                  pl.BlockSpec((None, DIFF_Q_ROWS, 3 * DIFF_Q_ROWS), lambda hd, bi: (hd, 0, 0)),
                  pl.BlockSpec((None, 1, LANES), lambda hd, bi: (layer, 0, 0))],
        out_specs=pl.BlockSpec((None, s, LANES), head_block),
        out_shape=jax.ShapeDtypeStruct((b, s, DIFF_HEADS * 2 * DIFF_HEAD_DIM), BF16),
        scratch_shapes=[pltpu.VMEM((s, 2 * LANES), BF16)],
        compiler_params=_params(2),
        name="diff",
    )(lam_vecs, far, qkv, near, subln)


def _post_body(h_ref, oa_ref, ob_ref, oc_ref, mgain_ref, wgate_ref, wo_ref, gain_ref, wg_ref, wu_ref,
               wd_ref, p_ref, pgain_ref, wpg_ref, wpp_ref, fin_ref, o_ref, *, final):
    d = D_MODEL
    halves = _row_halves(h_ref.shape[0])
    us = [_rms(h_ref[r, :], mgain_ref[...]).astype(BF16) for r in halves]
    sgs = [jax.nn.sigmoid(_dot(u, wgate_ref[...])) for u in us]
    sgs = [sg.astype(BF16) for sg in sgs]
    merged = [sg[:, :d] * oa_ref[r, :] + sg[:, d:2 * d] * ob_ref[r, :] + sg[:, 2 * d:] * oc_ref[r, :]
              for r, sg in zip(halves, sgs)]
    hs = [h_ref[r, :] + _dot(m, wo_ref[...]) for r, m in zip(halves, merged)]
    hs = _swiglu_half_step(hs, gain_ref, wg_ref, wu_ref, wd_ref)
    gates = [jax.nn.sigmoid(_dot(_rms(h, pgain_ref[...]).astype(BF16), wpg_ref[...])) for h in hs]
    for r, h, gate in zip(halves, hs, gates):
        h = h + gate * _dot(p_ref[r, :].astype(BF16), wpp_ref[...])
        o_ref[r, :] = _rms(h, fin_ref[...]) if final else h


def _post(h, layer, oa, ob, oc, mgain, wgate, w_out, gain, wg, wu, wd, p, pgain, wpg, wpp,
          final_norm, final):
    n = h.shape[0]
    tm = POST_ROWS
    row = lambda r: (r, 0)
    lay3 = lambda r: (layer, 0, 0)
    res = lambda a: _resident((None,) + a.shape[1:], lay3)
    return pl.pallas_call(
        functools.partial(_post_body, final=final),
        grid=(n // tm,),
        in_specs=[pl.BlockSpec((tm, D_MODEL), row), pl.BlockSpec((tm, D_MODEL), row),
                  pl.BlockSpec((tm, D_MODEL), row), pl.BlockSpec((tm, D_MODEL), row),
                  res(mgain), res(wgate), res(w_out), res(gain), res(wg), res(wu), res(wd),
                  pl.BlockSpec((None, tm, PLE_DIM), lambda r: (layer, r, 0)),
                  res(pgain), res(wpg), res(wpp), _resident((1, D_MODEL), lambda r: (0, 0))],
        out_specs=pl.BlockSpec((tm, D_MODEL), row),
        out_shape=jax.ShapeDtypeStruct(h.shape, F32),
        compiler_params=_params(1),
        name="post",
    )(h, oa, ob, oc, mgain, wgate, w_out, gain, wg, wu, wd, p, pgain, wpg, wpp, final_norm)


def _bias_body(table_ref, idx_ref, o_ref, *, first_head, n_heads):
    head = first_head + pl.program_id(0)
    idx = idx_ref[...]
    acc = jnp.full(idx.shape, NEG_INF, F32)
    for bucket in range(REL_BUCKETS):
        acc = jnp.where(idx == bucket, table_ref[bucket * n_heads + head] * LOG2E, acc)
    o_ref[...] = acc


def _bias_tiles(table_flat, idx, first_head, count, n_heads):
    return pl.pallas_call(
        functools.partial(_bias_body, first_head=first_head, n_heads=n_heads),
        grid=(count,),
        in_specs=[pl.BlockSpec(memory_space=pltpu.SMEM),
                  _resident(idx.shape, lambda hd: (0, 0))],
        out_specs=pl.BlockSpec((None,) + idx.shape, lambda hd: (hd, 0, 0)),
        out_shape=jax.ShapeDtypeStruct((count,) + idx.shape, F32),
        compiler_params=_params(1),
        name="bias",
    )(table_flat, idx)


def _t5_bucket_np(rel):
    nb = REL_BUCKETS // 2
    max_exact = nb // 2
    base = np.where(rel > 0, nb, 0)
    n = np.abs(rel)
    nf = np.maximum(n, 1).astype(np.float64)
    large = max_exact + (np.log(nf / max_exact) / math.log(REL_MAX_DIST / max_exact)
                         * (nb - max_exact)).astype(np.int64)
    large = np.minimum(large, nb - 1)
    return (base + np.where(n < max_exact, n, large)).astype(np.int32)


def _bias_indices():
    rel = (np.arange(3 * Q_BLOCK)[None, :] - WINDOW) - np.arange(Q_BLOCK)[:, None]
    idx_b = np.where(np.abs(rel) <= WINDOW, _t5_bucket_np(rel), REL_BUCKETS).astype(np.int32)
    t = DIFF_Q_ROWS
    rel = (np.arange(3 * t)[None, :] - t) - np.arange(t)[:, None]
    idx_c = _t5_bucket_np(rel)
    far = _t5_bucket_np(np.array([-t, t]))
    assert (far == _t5_bucket_np(np.array([-REL_MAX_DIST, REL_MAX_DIST]))).all()
    return idx_b, idx_c, far


def _rope_tables(seq):
    half = MLA_ROPE // 2
    inv = ROPE_THETA ** (-jnp.arange(0, MLA_ROPE, 2, dtype=F32) / MLA_ROPE)
    ang = jnp.arange(seq, dtype=F32)[:, None] * inv[None, :]
    cos, sin = jnp.cos(ang), jnp.sin(ang)
    z = lambda w: jnp.zeros((seq, w), F32)
    cos_t = jnp.concatenate([z(MLA_NOPE), cos, cos, z(LANES - MLA_NOPE - MLA_ROPE)], axis=1)
    sin_lo = jnp.concatenate([z(MLA_NOPE), -sin, z(LANES - MLA_NOPE - half)], axis=1)
    sin_hi = jnp.concatenate([z(MLA_NOPE + half), sin, z(LANES - MLA_NOPE - MLA_ROPE)], axis=1)
    return cos_t, sin_lo, sin_hi


def _dup_heads(w, heads, dim):
    w4 = w.reshape(w.shape[:2] + (heads, dim))
    return jnp.concatenate([w4, w4], axis=-1).reshape(w.shape[:2] + (heads * 2 * dim,))


def kernel(x, p, ffn1_norm, ffn1_w_gate, ffn1_w_up, ffn1_w_down, mix_norm, w_in, mla_q_norm, mla_w_uq, mla_kv_norm, mla_w_ukv, swa_sink, diff_lambda_q1, diff_lambda_k1, diff_lambda_q2, diff_lambda_k2, diff_subln, rel_table, w_out, ffn2_norm, ffn2_w_gate, ffn2_w_up, ffn2_w_down, ple_norm, ple_w_gate, ple_w_proj, final_norm):
    b, s, d = x.shape
    depth = w_in.shape[0]
    n = b * s
    bf = lambda a: a.astype(BF16)
    vec = lambda a: a.astype(F32)[:, None, :]

    o = np.cumsum((0, MLA_Q_LORA, MLA_KV_LORA, MLA_ROPE, 1024, 256, 256, 1024, 1024, 1024, 3072))
    w_in_bf = bf(w_in)
    col = lambda i: w_in_bf[:, :, o[i]:o[i + 1]]
    zeros = lambda w: jnp.zeros((depth, d, w), BF16)
    wlat = bf(jnp.concatenate([col(0), col(1), zeros(MLA_NOPE), col(2),
                               zeros(LANES - MLA_NOPE - MLA_ROPE)], axis=-1))
    by = lambda w, groups: w.reshape(depth, d, groups, w.shape[-1] // groups)
    wb = bf(jnp.concatenate([by(col(3), SWA_KV_HEADS),
                             by(_dup_heads(col(4), SWA_KV_HEADS, SWA_HEAD_DIM), SWA_KV_HEADS),
                             by(_dup_heads(col(5), SWA_KV_HEADS, SWA_HEAD_DIM), SWA_KV_HEADS)],
                            axis=-1).reshape(depth, d, SWA_KV_HEADS * SWA_STEP_COLS))
    wc = bf(jnp.concatenate([by(col(6), DIFF_HEADS), by(col(7), DIFF_HEADS), by(col(8), DIFF_HEADS)],
                            axis=-1).reshape(depth, d, DIFF_HEADS * DIFF_STEP_COLS))
    q_scale = lambda dim, q_cols, step_cols, steps: jnp.tile(jnp.concatenate(
        [jnp.full((q_cols,), dim ** -0.5 * LOG2E, F32), jnp.ones((step_cols - q_cols,), F32)]),
        steps)[None, :]
    sb = q_scale(SWA_HEAD_DIM, 2 * LANES, SWA_STEP_COLS, SWA_KV_HEADS)
    sc = q_scale(DIFF_HEAD_DIM, LANES, DIFF_STEP_COLS, DIFF_HEADS)
    wgate = bf(col(9))
    uq = mla_w_uq.reshape(depth, MLA_Q_LORA, MLA_HEADS, MLA_NOPE + MLA_ROPE)
    wuq = bf(jnp.pad(uq, ((0, 0),) * 3 + ((0, LANES - MLA_NOPE - MLA_ROPE),))
             .reshape(depth, MLA_Q_LORA, MLA_HEADS * LANES))
    ukv = mla_w_ukv.reshape(depth, MLA_KV_LORA, MLA_HEADS, MLA_NOPE + MLA_V)
    wuk = bf(jnp.pad(ukv[..., :MLA_NOPE], ((0, 0),) * 3 + ((0, LANES - MLA_NOPE),))
             .reshape(depth, MLA_KV_LORA, MLA_HEADS * LANES))
    wuv = bf(ukv[..., MLA_NOPE:].reshape(depth, MLA_KV_LORA, MLA_HEADS * MLA_V))

    cos_t, sin_lo, sin_hi = _rope_tables(s)
    idx_b, idx_c, far_buckets = _bias_indices()
    n_bias_heads = SWA_HEADS + DIFF_HEADS
    table_flat = rel_table.astype(F32).reshape(REL_BUCKETS * n_bias_heads)
    per_kv = SWA_HEADS // SWA_KV_HEADS
    bias_b = _bias_tiles(table_flat, jnp.asarray(idx_b), 0, SWA_HEADS, n_bias_heads).reshape(
        SWA_KV_HEADS, per_kv * Q_BLOCK, 3 * Q_BLOCK)
    near_c = _bias_tiles(table_flat, jnp.asarray(idx_c), SWA_HEADS, DIFF_HEADS, n_bias_heads)
    far_c = (jnp.stack([rel_table[int(far_buckets[0]), SWA_HEADS:],
                        rel_table[int(far_buckets[1]), SWA_HEADS:]], axis=1).astype(F32)
             * LOG2E).reshape(2 * DIFF_HEADS)
    sink = swa_sink.astype(F32) * LOG2E
    lam_vecs = jnp.stack([diff_lambda_q1, diff_lambda_k1, diff_lambda_q2, diff_lambda_k2],
                         axis=1).astype(F32)
    f1 = (vec(ffn1_norm), bf(ffn1_w_gate), bf(ffn1_w_up), bf(ffn1_w_down))
    f2 = (vec(ffn2_norm), bf(ffn2_w_gate), bf(ffn2_w_up), bf(ffn2_w_down))
    mixn, qn, kvn, subln, plen = (vec(mix_norm), vec(mla_q_norm), vec(mla_kv_norm),
                                  vec(diff_subln), vec(ple_norm))
    wo, wpg, wpp = bf(w_out), bf(ple_w_gate), bf(ple_w_proj)
    fin = final_norm.astype(F32)[None, :]
    p2 = p.reshape(depth, n, PLE_DIM)

    h = x.reshape(n, d).astype(F32)
    for i in range(depth):
        h = _ffn(h, i, *f1)
        qkv_a, qkv_b, qkv_c = _proj(h, i, s, mixn, wlat, wb, wc, qn, kvn, wuq, wuk, wuv,
                                    sb, sc, cos_t, sin_lo, sin_hi)
        r3 = lambda a: a.reshape(b, s, a.shape[-1])
        oa = _mla(r3(qkv_a))
        ob = _swa(sink[i], r3(qkv_b), bias_b)
        lambda_init = 0.8 - 0.6 * math.exp(-0.3 * i)
        oc = _diff(i, lambda_init, lam_vecs, far_c, r3(qkv_c), near_c, subln)
        h = _post(h, i, oa.reshape(n, d), ob.reshape(n, d), oc.reshape(n, d), mixn, wgate, wo, *f2,
                  p2, plen, wpg, wpp, fin, final=(i == depth - 1))
    return h.reshape(b, s, d).astype(x.dtype)
```
